```python
import jax, jax.numpy as jnp
from jax import lax
import numpy as np

D_MODEL = 1024
BATCH = 16
SEQ = 2048
DEPTH = 2

D_MIX = 2 * D_MODEL
D_ML = D_MIX // 2
N_ML_HEADS = 4
ML_HEAD_DIM = D_ML // N_ML_HEADS
QK_CONV = 4
ML_CHUNK = 128
D_CV = D_MIX - D_ML
CV_KERNEL = 31
N_IN = 2 * D_ML + D_ML + D_ML + 2 * N_ML_HEADS + 2 * D_CV
D_FF = ((8 * D_MODEL // 3 + 127) // 128) * 128
FFN_KERNEL = 3
N_MOD = 6
EPS = 1e-6

kernel_name = "hymba_style_mlstm_conformer_convffn_block"


def rms_norm(x, g):
    xf = x.astype(jnp.float32)
    y = xf * lax.rsqrt(jnp.mean(xf * xf, axis=-1, keepdims=True) + EPS)
    return (y * g.astype(jnp.float32)).astype(x.dtype)


def layer_norm(x, g, b):
    xf = x.astype(jnp.float32)
    mu = jnp.mean(xf, axis=-1, keepdims=True)
    var = jnp.mean(jnp.square(xf - mu), axis=-1, keepdims=True)
    y = (xf - mu) * lax.rsqrt(var + EPS)
    return (y * g.astype(jnp.float32) + b.astype(jnp.float32)).astype(x.dtype)


def causal_dwconv(x, w, b):
    K, C = w.shape
    y = lax.conv_general_dilated(
        x, w[:, None, :].astype(x.dtype), window_strides=(1,), padding=[(K - 1, 0)],
        dimension_numbers=("NWC", "WIO", "NWC"), feature_group_count=C)
    return y + b


def mlstm_chunkwise(q, k, v, i_pre, f_pre):
    Bsz, S, H, Dh = q.shape
    L = ML_CHUNK
    nc = S // L

    def chunks(t):
        t = t.astype(jnp.float32).reshape((Bsz, nc, L, H) + t.shape[3:])
        return jnp.moveaxis(t, (1, 3), (0, 2))

    qc = chunks(q)
    kc = chunks(k) * (Dh ** -0.5)
    vc = chunks(v)
    ic = chunks(i_pre)
    lfc = jax.nn.log_sigmoid(chunks(f_pre))
    causal = jnp.tril(jnp.ones((L, L), dtype=bool))

    def step(carry, inp):
        C, n, m = carry
        q_, k_, v_, i_, lf_ = inp
        b = jnp.cumsum(lf_, axis=-1)
        dmat = jnp.where(causal, b[..., :, None] - b[..., None, :] + i_[..., None, :], -jnp.inf)
        m_t = jnp.maximum(b + m[..., None], jnp.max(dmat, axis=-1))
        w_inter = jnp.exp(b + m[..., None] - m_t)
        scores = jnp.einsum("bhtd,bhsd->bhts", q_, k_) * jnp.exp(dmat - m_t[..., None])
        num = (w_inter[..., None] * jnp.einsum("bhtd,bhde->bhte", q_, C)
               + jnp.einsum("bhts,bhse->bhte", scores, v_))
        den = w_inter * jnp.einsum("bhtd,bhd->bht", q_, n) + jnp.sum(scores, axis=-1)
        h = num / jnp.maximum(jnp.abs(den), jnp.exp(-m_t))[..., None]
        b_last = b[..., -1]
        g = b_last[..., None] - b + i_
        m_new = jnp.maximum(b_last + m, jnp.max(g, axis=-1))
        decay = jnp.exp(b_last + m - m_new)
        kw = k_ * jnp.exp(g - m_new[..., None])[..., None]
        C = decay[..., None, None] * C + jnp.einsum("bhsd,bhse->bhde", kw, v_)
        n = decay[..., None] * n + jnp.sum(kw, axis=2)
        return (C, n, m_new), h

    init = (jnp.zeros((Bsz, H, Dh, Dh), jnp.float32),
            jnp.zeros((Bsz, H, Dh), jnp.float32),
            jnp.zeros((Bsz, H), jnp.float32))
    _, h = lax.scan(step, init, (qc, kc, vc, ic, lfc))
    return jnp.moveaxis(h, (0, 2), (1, 3)).reshape(Bsz, S, H, Dh)


def setup_inputs(seed: int = 0) -> dict:
    key = jax.random.key(seed)
    ks = jax.random.split(key, 24)

    def nrm(k, shape, scale):
        return jax.random.normal(k, shape, jnp.float32) * scale

    def gain(k, shape):
        return 1.0 + nrm(k, shape, 0.05)

    Ly = DEPTH
    fbias = jnp.linspace(3.0, 6.0, N_ML_HEADS, dtype=jnp.float32)[None, :] + nrm(ks[9], (Ly, N_ML_HEADS), 0.1)
    return {
        "x": nrm(ks[0], (BATCH, SEQ, D_MODEL), 1.0),
        "c": nrm(ks[1], (BATCH, D_MODEL), 1.0),
        "ada_w": nrm(ks[2], (Ly, D_MODEL, N_MOD * D_MODEL), 0.5 * D_MODEL ** -0.5),
        "ada_b": nrm(ks[3], (Ly, N_MOD * D_MODEL), 0.02),
        "mix_pre_g": gain(ks[4], (Ly, D_MODEL)),
        "mix_post_g": gain(ks[5], (Ly, D_MODEL)),
        "w_in": nrm(ks[6], (Ly, D_MODEL, N_IN), D_MODEL ** -0.5),
        "qk_conv_w": nrm(ks[7], (Ly, QK_CONV, 2 * D_ML), QK_CONV ** -0.5),
        "qk_conv_b": nrm(ks[8], (Ly, 2 * D_ML), 0.02),
        "igate_b": nrm(ks[10], (Ly, N_ML_HEADS), 0.1),
        "fgate_b": fbias,
        "ml_norm_g": gain(ks[11], (Ly, D_ML)),
        "cv_dw_w": nrm(ks[12], (Ly, CV_KERNEL, D_CV), CV_KERNEL ** -0.5),
        "cv_dw_b": nrm(ks[13], (Ly, D_CV), 0.02),
        "cv_ln_g": gain(ks[14], (Ly, D_CV)),
        "cv_ln_b": nrm(ks[15], (Ly, D_CV), 0.02),
        "w_out": nrm(ks[16], (Ly, D_MIX, D_MODEL), D_MIX ** -0.5),
        "ffn_pre_g": gain(ks[17], (Ly, D_MODEL)),
        "ffn_post_g": gain(ks[18], (Ly, D_MODEL)),
        "ffn_up": nrm(ks[19], (Ly, D_MODEL, 2 * D_FF), D_MODEL ** -0.5),
        "ffn_conv_w": nrm(ks[20], (Ly, FFN_KERNEL, 2 * D_FF), FFN_KERNEL ** -0.5),
        "ffn_conv_b": nrm(ks[21], (Ly, 2 * D_FF), 0.02),
        "ffn_down": nrm(ks[22], (Ly, D_FF, D_MODEL), D_FF ** -0.5),
    }


def reference(x, c, ada_w, ada_b, mix_pre_g, mix_post_g, w_in, qk_conv_w, qk_conv_b,
              igate_b, fgate_b, ml_norm_g, cv_dw_w, cv_dw_b, cv_ln_g, cv_ln_b, w_out,
              ffn_pre_g, ffn_post_g, ffn_up, ffn_conv_w, ffn_conv_b, ffn_down):
    Bsz, S, _ = x.shape
    H, Dh = N_ML_HEADS, ML_HEAD_DIM
    cond = jax.nn.silu(c)
    for l in range(DEPTH):
        mod = cond @ ada_w[l] + ada_b[l]
        sh1, sc1, g1, sh2, sc2, g2 = jnp.split(mod[:, None, :], N_MOD, axis=-1)

        u = rms_norm(x, mix_pre_g[l]) * (1.0 + sc1) + sh1
        proj = u @ w_in[l]
        qk_raw, v, o_pre, gates, glu = jnp.split(
            proj, [2 * D_ML, 3 * D_ML, 4 * D_ML, 4 * D_ML + 2 * H], axis=-1)

        qk = jax.nn.silu(causal_dwconv(qk_raw, qk_conv_w[l], qk_conv_b[l]))
        q, k = jnp.split(qk, 2, axis=-1)
        heads = lambda t: t.reshape(Bsz, S, H, Dh)
        h = mlstm_chunkwise(heads(q), heads(k), heads(v),
                            gates[..., :H] + igate_b[l], gates[..., H:] + fgate_b[l])
        h = rms_norm(h, ml_norm_g[l].reshape(H, Dh)).reshape(Bsz, S, D_ML)
        h = (h * jax.nn.sigmoid(o_pre.astype(jnp.float32))).astype(x.dtype)

        a, gt = jnp.split(glu, 2, axis=-1)
        y = causal_dwconv(a * jax.nn.sigmoid(gt), cv_dw_w[l], cv_dw_b[l])
        y = jax.nn.silu(layer_norm(y, cv_ln_g[l], cv_ln_b[l]))

        mix = jnp.concatenate([h, y], axis=-1) @ w_out[l]
        x = x + g1 * rms_norm(mix, mix_post_g[l])

        u = rms_norm(x, ffn_pre_g[l]) * (1.0 + sc2) + sh2
        up = causal_dwconv(u @ ffn_up[l], ffn_conv_w[l], ffn_conv_b[l])
        a, gt = jnp.split(up, 2, axis=-1)
        f = (jax.nn.silu(gt) * a) @ ffn_down[l]
        x = x + g2 * rms_norm(f, ffn_post_g[l])
    return x
```

```python
import functools

import jax
import jax.numpy as jnp
from jax import lax
from jax.experimental import pallas as pl
from jax.experimental.pallas import tpu as pltpu

D_MODEL = 1024
N_HEADS = 4
HEAD_DIM = 256
D_ML = N_HEADS * HEAD_DIM
D_CV = 1024
QK_CONV = 4
CV_KERNEL = 31
ML_CHUNK = 128
D_FF = 2816
FFN_KERNEL = 3
N_MOD = 6
EPS = 1e-6

LANES = 128
SUBLANES = 8
SEQ_TILE = 512
QK_HALO = SUBLANES
CV_HALO = 4 * SUBLANES
CV_ROWS = 64
COL_BLOCK = 512
FFN_COL_BLOCK = 256
VMEM_LIMIT_BYTES = 58 * 1024 * 1024

_F32 = jnp.float32
_BF16 = jnp.bfloat16
_NEG_INF = float("-inf")


def _dot(a, b):
    return jnp.dot(a, b, preferred_element_type=_F32)


def _sigmoid(x):
    return 1.0 / (1.0 + jnp.exp(-x))


def _rms_scale(x):
    return lax.rsqrt(jnp.mean(x * x, axis=-1, keepdims=True) + EPS)


def _mod_kernel(c_ref, w_ref, b_ref, o_ref):
    c = c_ref[...]
    cond = (c * _sigmoid(c)).astype(_BF16)
    o_ref[0] = _dot(cond, w_ref[0].astype(_BF16)) + b_ref[0]


def _modulation(c, ada_w, ada_b):
    depth, d, n = ada_w.shape
    bsz = c.shape[0]
    nb = n // d
    return pl.pallas_call(
        _mod_kernel,
        grid=(depth, nb),
        in_specs=[
            pl.BlockSpec((bsz, d), lambda l, j: (0, 0)),
            pl.BlockSpec((1, d, d), lambda l, j: (l, 0, j)),
            pl.BlockSpec((1, 1, d), lambda l, j: (l, 0, j)),
        ],
        out_specs=pl.BlockSpec((1, bsz, d), lambda l, j: (l, 0, j)),
        out_shape=jax.ShapeDtypeStruct((depth, bsz, n), _F32),
        compiler_params=pltpu.CompilerParams(
            dimension_semantics=("arbitrary", "arbitrary")),
        name="modulation",
    )(c, ada_w, ada_b.reshape(depth, 1, n))


def _mixer_kernel(x_ref, mod_ref, pre_g_ref, post_g_ref, w_qk_ref, w_v_ref, w_o_ref,
                  w_gate_ref, gate_b_ref, w_glu_ref, qkc_w_ref, qkc_b_ref, mln_g_ref,
                  cv_w_ref, cv_b_ref, ln_g_ref, ln_b_ref, w_out_ref, out_ref,
                  u_s, qk_buf, q_s, k_s, v_s, o_s, z_s, cv_buf, y_s, hy_s, c_s, n_s, m_s):
    T = SEQ_TILE
    D = D_MODEL
    t_idx = pl.program_id(1)

    @pl.when(t_idx == 0)
    def _():
        qk_buf[0:QK_HALO, :] = jnp.zeros((QK_HALO, 2 * D_ML), _F32)
        cv_buf[:, 0:CV_HALO, :] = jnp.zeros((D_CV // LANES, CV_HALO, LANES), _F32)
        c_s[...] = jnp.zeros(c_s.shape, _F32)
        n_s[...] = jnp.zeros(n_s.shape, _F32)
        m_s[...] = jnp.zeros(m_s.shape, _F32)

    x = x_ref[0]
    mod = mod_ref[0]
    sh1 = mod[:, 0:D]
    sc1 = mod[:, D:2 * D]
    g1 = mod[:, 2 * D:3 * D]
    u = (x * _rms_scale(x)) * pre_g_ref[...] * (1.0 + sc1) + sh1
    u_s[...] = u.astype(_BF16)

    for j in range(2 * D_ML // COL_BLOCK):
        cols = slice(j * COL_BLOCK, (j + 1) * COL_BLOCK)
        qk_buf[QK_HALO:QK_HALO + T, cols] = _dot(u_s[...], w_qk_ref[:, cols])
    for j in range(D_ML // COL_BLOCK):
        cols = slice(j * COL_BLOCK, (j + 1) * COL_BLOCK)
        v_s[:, cols] = _dot(u_s[...], w_v_ref[:, cols]).astype(_BF16)
        o_s[:, cols] = _dot(u_s[...], w_o_ref[:, cols])
    gates = _dot(u_s[...], w_gate_ref[...]) + gate_b_ref[...]
    lane = lax.broadcasted_iota(jnp.int32, (T, LANES), 1)
    log_f = jnp.minimum(gates, 0.0) - jnp.log1p(jnp.exp(-jnp.abs(gates)))
    z_s[...] = jnp.where(lane < N_HEADS, gates, log_f)
    for j in range(D_CV // COL_BLOCK):
        cols = slice(j * COL_BLOCK, (j + 1) * COL_BLOCK)
        a = _dot(u_s[...], w_glu_ref[:, cols])
        gt = _dot(u_s[...], w_glu_ref[:, D_CV + j * COL_BLOCK:D_CV + (j + 1) * COL_BLOCK])
        glu = a * _sigmoid(gt)
        for jj in range(COL_BLOCK // LANES):
            cv_buf[j * (COL_BLOCK // LANES) + jj, CV_HALO:CV_HALO + T, :] = (
                glu[:, jj * LANES:(jj + 1) * LANES])

    off = QK_HALO - (QK_CONV - 1)
    for j in range(2 * D_ML // COL_BLOCK):
        cols = slice(j * COL_BLOCK, (j + 1) * COL_BLOCK)
        acc = qkc_b_ref[:, cols] + qkc_w_ref[0:1, cols] * qk_buf[off:off + T, cols]
        for k in range(1, QK_CONV):
            acc = acc + qkc_w_ref[k:k + 1, cols] * qk_buf[off + k:off + k + T, cols]
        act = acc * _sigmoid(acc)
        if j < D_ML // COL_BLOCK:
            q_s[:, cols] = act.astype(_BF16)
        else:
            kcols = slice(j * COL_BLOCK - D_ML, (j + 1) * COL_BLOCK - D_ML)
            k_s[:, kcols] = act * (HEAD_DIM ** -0.5)
    qk_buf[0:QK_HALO, :] = qk_buf[T:T + QK_HALO, :]

    cv_off = CV_HALO - (CV_KERNEL - 1)

    def cv_lane_tile(j, carry):
        def cv_rows(i, carry2):
            r0 = pl.multiple_of(i * CV_ROWS, CV_ROWS)
            acc = cv_b_ref[j] + cv_w_ref[j, 0:1, :] * cv_buf[j, pl.ds(r0 + cv_off, CV_ROWS), :]
            for k in range(1, CV_KERNEL):
                acc = acc + (cv_w_ref[j, k:k + 1, :]
                             * cv_buf[j, pl.ds(r0 + cv_off + k, CV_ROWS), :])
            y_s[j, pl.ds(r0, CV_ROWS), :] = acc
            return carry2

        lax.fori_loop(0, T // CV_ROWS, cv_rows, 0)
        return carry

    lax.fori_loop(0, D_CV // LANES, cv_lane_tile, 0)
    cv_buf[:, 0:CV_HALO, :] = cv_buf[:, T:T + CV_HALO, :]

    n_lt = D_CV // LANES
    tot = y_s[0]
    for j in range(1, n_lt):
        tot = tot + y_s[j]
    mu = jnp.sum(tot, axis=-1, keepdims=True) * (1.0 / D_CV)
    sq = jnp.square(y_s[0] - mu)
    for j in range(1, n_lt):
        sq = sq + jnp.square(y_s[j] - mu)
    var = jnp.sum(sq, axis=-1, keepdims=True) * (1.0 / D_CV)
    inv = lax.rsqrt(var + EPS)
    for j in range(n_lt):
        lcols = slice(j * LANES, (j + 1) * LANES)
        yn = (y_s[j] - mu) * inv * ln_g_ref[:, lcols] + ln_b_ref[:, lcols]
        hy_s[:, D_ML + j * LANES:D_ML + (j + 1) * LANES] = (yn * _sigmoid(yn)).astype(_BF16)

    L = ML_CHUNK
    row_i = lax.broadcasted_iota(jnp.int32, (L, L), 0)
    col_i = lax.broadcasted_iota(jnp.int32, (L, L), 1)
    causal = row_i >= col_i
    tril = causal.astype(_F32)
    lane_l = lax.broadcasted_iota(jnp.int32, (L, LANES), 1)

    def chunk_step(c, carry):
        r0 = pl.multiple_of(c * L, L)
        rows = pl.ds(r0, L)
        z = z_s[rows, :]
        csum = jnp.dot(tril, z, precision=lax.Precision.HIGHEST,
                       preferred_element_type=_F32)
        zb = jnp.where(lane_l < N_HEADS, z, csum)
        zb_t = zb.T
        for h in range(N_HEADS):
            hc = slice(h * HEAD_DIM, (h + 1) * HEAD_DIM)
            i_col = zb[:, h:h + 1]
            b_col = zb[:, N_HEADS + h:N_HEADS + h + 1]
            i_row = zb_t[h:h + 1, :]
            b_row = zb_t[N_HEADS + h:N_HEADS + h + 1, :]
            m_prev = m_s[h, 0:1, 0:1]
            q = q_s[rows, hc]
            k = k_s[rows, hc]
            v = v_s[rows, hc]
            dmat = jnp.where(causal, b_col - b_row + i_row, _NEG_INF)
            m_t = jnp.maximum(b_col + m_prev, jnp.max(dmat, axis=-1, keepdims=True))
            w_inter = jnp.exp(b_col + m_prev - m_t)
            s = lax.dot_general(q, k.astype(_BF16), (((1,), (1,)), ((), ())),
                                preferred_element_type=_F32) * jnp.exp(dmat - m_t)
            c_prev = c_s[h]
            num = w_inter * _dot(q, c_prev.astype(_BF16)) + _dot(s.astype(_BF16), v)
            qn = jnp.sum(q.astype(_F32) * n_s[h:h + 1, :], axis=-1, keepdims=True)
            den = w_inter * qn + jnp.sum(s, axis=-1, keepdims=True)
            hh = num * (1.0 / jnp.maximum(jnp.abs(den), jnp.exp(-m_t)))
            hn = hh * _rms_scale(hh) * mln_g_ref[:, hc]
            hy_s[rows, hc] = (hn * _sigmoid(o_s[rows, hc])).astype(_BF16)
            b_last = b_col[L - 1:L, :]
            g = b_last - b_col + i_col
            m_new = jnp.maximum(b_last + m_prev, jnp.max(g, axis=0, keepdims=True))
            decay = jnp.exp(b_last + m_prev - m_new)
            kw = k * jnp.exp(g - m_new)
            c_s[h] = decay * c_prev + lax.dot_general(
                kw.astype(_BF16), v, (((0,), (0,)), ((), ())), preferred_element_type=_F32)
            n_s[h:h + 1, :] = decay * n_s[h:h + 1, :] + jnp.sum(kw, axis=0, keepdims=True)
            m_s[h] = jnp.broadcast_to(m_new, (SUBLANES, LANES))
        return carry

    lax.fori_loop(0, T // L, chunk_step, 0)

    mix = _dot(hy_s[...], w_out_ref[...])
    out_ref[0] = x + g1 * (mix * _rms_scale(mix) * post_g_ref[...])


def _const_spec(shape):
    nd = len(shape)
    return pl.BlockSpec(shape, lambda b, t: (0,) * nd, pipeline_mode=pl.Buffered(1))


def _mixer(x, mod, pre_g, post_g, w_in, qkc_w, qkc_b, igate_b, fgate_b, mln_g,
           cv_w, cv_b, ln_g, ln_b, w_out):
    bsz, seq, d = x.shape
    T = SEQ_TILE
    w_qk = w_in[:, 0:2 * D_ML].astype(_BF16)
    w_v = w_in[:, 2 * D_ML:3 * D_ML].astype(_BF16)
    w_o = w_in[:, 3 * D_ML:4 * D_ML].astype(_BF16)
    n_gate = 2 * N_HEADS
    w_gate = jnp.pad(w_in[:, 4 * D_ML:4 * D_ML + n_gate],
                     ((0, 0), (0, LANES - n_gate))).astype(_BF16)
    gate_b = jnp.pad(jnp.concatenate([igate_b, fgate_b]), (0, LANES - n_gate)).reshape(1, LANES)
    w_glu = w_in[:, 4 * D_ML + n_gate:].astype(_BF16)
    n_lt = D_CV // LANES
    cv_w3 = jnp.pad(cv_w, ((0, 4 * SUBLANES - CV_KERNEL), (0, 0)))
    cv_w3 = cv_w3.reshape(4 * SUBLANES, n_lt, LANES).transpose(1, 0, 2)
    cv_b3 = cv_b.reshape(n_lt, 1, LANES)
    qkc_w8 = jnp.pad(qkc_w, ((0, SUBLANES - QK_CONV), (0, 0)))

    operands = [
        (x, pl.BlockSpec((1, T, d), lambda b, t: (b, t, 0))),
        (mod, pl.BlockSpec((1, 1, N_MOD * d), lambda b, t: (b, 0, 0))),
        (pre_g.reshape(1, d), None),
        (post_g.reshape(1, d), None),
        (w_qk, None), (w_v, None), (w_o, None), (w_gate, None), (gate_b, None), (w_glu, None),
        (qkc_w8, None), (qkc_b.reshape(1, 2 * D_ML), None), (mln_g.reshape(1, D_ML), None),
        (cv_w3, None), (cv_b3, None), (ln_g.reshape(1, D_CV), None), (ln_b.reshape(1, D_CV), None),
        (w_out.astype(_BF16), None),
    ]
    args = [a for a, _ in operands]
    in_specs = [s if s is not None else _const_spec(a.shape) for a, s in operands]
    scratch = [
        pltpu.VMEM((T, d), _BF16),
        pltpu.VMEM((QK_HALO + T, 2 * D_ML), _F32),
        pltpu.VMEM((T, D_ML), _BF16),
        pltpu.VMEM((T, D_ML), _F32),
        pltpu.VMEM((T, D_ML), _BF16),
        pltpu.VMEM((T, D_ML), _F32),
        pltpu.VMEM((T, LANES), _F32),
        pltpu.VMEM((n_lt, CV_HALO + T, LANES), _F32),
        pltpu.VMEM((n_lt, T, LANES), _F32),
        pltpu.VMEM((T, D_ML + D_CV), _BF16),
        pltpu.VMEM((N_HEADS, HEAD_DIM, HEAD_DIM), _F32),
        pltpu.VMEM((SUBLANES, HEAD_DIM), _F32),
        pltpu.VMEM((N_HEADS, SUBLANES, LANES), _F32),
    ]
    return pl.pallas_call(
        _mixer_kernel,
        grid=(bsz, seq // T),
        in_specs=in_specs,
        out_specs=pl.BlockSpec((1, T, d), lambda b, t: (b, t, 0)),
        out_shape=jax.ShapeDtypeStruct(x.shape, x.dtype),
        scratch_shapes=scratch,
        compiler_params=pltpu.CompilerParams(
            dimension_semantics=("arbitrary", "arbitrary"),
            vmem_limit_bytes=VMEM_LIMIT_BYTES),
        name="mixer",
    )(*args)


def _ffn_kernel(x_ref, mod_ref, pre_g_ref, post_g_ref, up_ref, cw_ref, cb_ref, down_ref,
                out_ref, u_s, buf_a, buf_g, halo_s, f_s):
    T = SEQ_TILE
    D = D_MODEL
    W = FFN_COL_BLOCK
    t_idx = pl.program_id(1)

    @pl.when(t_idx == 0)
    def _():
        halo_s[...] = jnp.zeros(halo_s.shape, _F32)

    x = x_ref[0]
    mod = mod_ref[0]
    sh2 = mod[:, 3 * D:4 * D]
    sc2 = mod[:, 4 * D:5 * D]
    g2 = mod[:, 5 * D:6 * D]
    u = (x * _rms_scale(x)) * pre_g_ref[...] * (1.0 + sc2) + sh2
    u_s[...] = u.astype(_BF16)

    off = QK_HALO - (FFN_KERNEL - 1)

    def conv(buf, cols):
        acc = cb_ref[:, cols] + cw_ref[0:1, cols] * buf[off:off + T, :]
        for k in range(1, FFN_KERNEL):
            acc = acc + cw_ref[k:k + 1, cols] * buf[off + k:off + k + T, :]
        return acc

    for j in range(D_FF // W):
        cols_a = slice(j * W, (j + 1) * W)
        cols_g = slice(D_FF + j * W, D_FF + (j + 1) * W)
        for buf, cols in ((buf_a, cols_a), (buf_g, cols_g)):
            buf[0:QK_HALO, :] = halo_s[:, cols]
            buf[QK_HALO:QK_HALO + T, :] = _dot(u_s[...], up_ref[:, cols])
            halo_s[:, cols] = buf[T:T + QK_HALO, :]
        a = conv(buf_a, cols_a)
        gt = conv(buf_g, cols_g)
        f_s[:, cols_a] = (gt * _sigmoid(gt) * a).astype(_BF16)

    f = _dot(f_s[...], down_ref[...])
    out_ref[0] = x + g2 * (f * _rms_scale(f) * post_g_ref[...])


def _ffn(x, mod, pre_g, post_g, up, conv_w, conv_b, down):
    bsz, seq, d = x.shape
    T = SEQ_TILE
    cw8 = jnp.pad(conv_w, ((0, SUBLANES - FFN_KERNEL), (0, 0)))
    operands = [
        (x, pl.BlockSpec((1, T, d), lambda b, t: (b, t, 0))),
        (mod, pl.BlockSpec((1, 1, N_MOD * d), lambda b, t: (b, 0, 0))),
        (pre_g.reshape(1, d), None),
        (post_g.reshape(1, d), None),
        (up.astype(_BF16), None),
        (cw8, None),
        (conv_b.reshape(1, 2 * D_FF), None),
        (down.astype(_BF16), None),
    ]
    args = [a for a, _ in operands]
    in_specs = [s if s is not None else _const_spec(a.shape) for a, s in operands]
    scratch = [
        pltpu.VMEM((T, d), _BF16),
        pltpu.VMEM((QK_HALO + T, FFN_COL_BLOCK), _F32),
        pltpu.VMEM((QK_HALO + T, FFN_COL_BLOCK), _F32),
        pltpu.VMEM((QK_HALO, 2 * D_FF), _F32),
        pltpu.VMEM((T, D_FF), _BF16),
    ]
    return pl.pallas_call(
        _ffn_kernel,
        grid=(bsz, seq // T),
        in_specs=in_specs,
        out_specs=pl.BlockSpec((1, T, d), lambda b, t: (b, t, 0)),
        out_shape=jax.ShapeDtypeStruct(x.shape, x.dtype),
        scratch_shapes=scratch,
        compiler_params=pltpu.CompilerParams(
            dimension_semantics=("arbitrary", "arbitrary"),
            vmem_limit_bytes=VMEM_LIMIT_BYTES),
        name="ffn",
    )(*args)


def kernel(x, c, ada_w, ada_b, mix_pre_g, mix_post_g, w_in, qk_conv_w, qk_conv_b, igate_b, fgate_b, ml_norm_g, cv_dw_w, cv_dw_b, cv_ln_g, cv_ln_b, w_out, ffn_pre_g, ffn_post_g, ffn_up, ffn_conv_w, ffn_conv_b, ffn_down):
    bsz, seq, d = x.shape
    depth = ada_w.shape[0]
    assert d == D_MODEL and seq % SEQ_TILE == 0
    mod_all = _modulation(c, ada_w, ada_b).reshape(depth, bsz, 1, N_MOD * d)
    for l in range(depth):
        mod = mod_all[l]
        x = _mixer(x, mod, mix_pre_g[l], mix_post_g[l], w_in[l], qk_conv_w[l], qk_conv_b[l],
                   igate_b[l], fgate_b[l], ml_norm_g[l], cv_dw_w[l], cv_dw_b[l],
                   cv_ln_g[l], cv_ln_b[l], w_out[l])
        x = _ffn(x, mod, ffn_pre_g[l], ffn_post_g[l], ffn_up[l], ffn_conv_w[l],
                 ffn_conv_b[l], ffn_down[l])
    return x
```

```python
import functools

import jax
import jax.numpy as jnp
from jax import lax
from jax.experimental import pallas as pl
from jax.experimental.pallas import tpu as pltpu

D_MODEL = 1024
N_HEADS = 4
HEAD_DIM = 256
D_ML = N_HEADS * HEAD_DIM
D_CV = 1024
QK_CONV = 4
CV_KERNEL = 31
ML_CHUNK = 128
D_FF = 2816
FFN_KERNEL = 3
N_MOD = 6
EPS = 1e-6

LANES = 128
SUBLANES = 8
SEQ_TILE = 512
QK_HALO = SUBLANES
CV_HALO = 4 * SUBLANES
CV_ROWS = 64
COL_BLOCK = 512
FFN_COL_BLOCK = 256
VMEM_LIMIT_BYTES = 58 * 1024 * 1024

_F32 = jnp.float32
_BF16 = jnp.bfloat16
_NEG_INF = float("-inf")


def _dot(a, b):
    return jnp.dot(a, b, preferred_element_type=_F32)


def _sigmoid(x):
    return 1.0 / (1.0 + jnp.exp(-x))


def _rms_scale(x):
    return lax.rsqrt(jnp.mean(x * x, axis=-1, keepdims=True) + EPS)


def _mod_kernel(c_ref, w_ref, b_ref, o_ref):
    c = c_ref[...]
    cond = (c * _sigmoid(c)).astype(_BF16)
    o_ref[0] = _dot(cond, w_ref[0].astype(_BF16)) + b_ref[0]


def _modulation(c, ada_w, ada_b):
    depth, d, n = ada_w.shape
    bsz = c.shape[0]
    nb = n // d
    return pl.pallas_call(
        _mod_kernel,
        grid=(depth, nb),
        in_specs=[
            pl.BlockSpec((bsz, d), lambda l, j: (0, 0)),
            pl.BlockSpec((1, d, d), lambda l, j: (l, 0, j)),
            pl.BlockSpec((1, 1, d), lambda l, j: (l, 0, j)),
        ],
        out_specs=pl.BlockSpec((1, bsz, d), lambda l, j: (l, 0, j)),
        out_shape=jax.ShapeDtypeStruct((depth, bsz, n), _F32),
        compiler_params=pltpu.CompilerParams(
            dimension_semantics=("arbitrary", "arbitrary")),
        name="modulation",
    )(c, ada_w, ada_b.reshape(depth, 1, n))


def _mixer_kernel(x_ref, mod_ref, pre_g_ref, post_g_ref, w_qk_ref, w_v_ref, w_o_ref,
                  w_gate_ref, gate_b_ref, w_glu_ref, qkc_w_ref, qkc_b_ref, mln_g_ref,
                  cv_w_ref, cv_b_ref, ln_g_ref, ln_b_ref, w_out_ref, out_ref,
                  u_s, qk_buf, q_s, k_s, v_s, o_s, z_s, cv_buf, y_s, hy_s, c_s, n_s, m_s):
    T = SEQ_TILE
    D = D_MODEL
    t_idx = pl.program_id(1)

    lt_per_block = COL_BLOCK // LANES

    def store_lane_tiles(buf, j, halo, val):
        for jj in range(lt_per_block):
            buf[j * lt_per_block + jj, halo:halo + T, :] = val[:, jj * LANES:(jj + 1) * LANES]

    @pl.when(t_idx == 0)
    def _():
        qk_buf[:, 0:QK_HALO, :] = jnp.zeros((2 * D_ML // LANES, QK_HALO, LANES), _F32)
        cv_buf[:, 0:CV_HALO, :] = jnp.zeros((D_CV // LANES, CV_HALO, LANES), _F32)
        c_s[...] = jnp.zeros(c_s.shape, _F32)
        n_s[...] = jnp.zeros(n_s.shape, _F32)
        m_s[...] = jnp.zeros(m_s.shape, _F32)

    x = x_ref[0]
    mod = mod_ref[0]
    sh1 = mod[:, 0:D]
    sc1 = mod[:, D:2 * D]
    g1 = mod[:, 2 * D:3 * D]
    u = (x * _rms_scale(x)) * pre_g_ref[...] * (1.0 + sc1) + sh1
    u_s[...] = u.astype(_BF16)

    for j in range(2 * D_ML // COL_BLOCK):
        cols = slice(j * COL_BLOCK, (j + 1) * COL_BLOCK)
        store_lane_tiles(qk_buf, j, QK_HALO, _dot(u_s[...], w_qk_ref[:, cols]))
    for j in range(D_ML // COL_BLOCK):
        cols = slice(j * COL_BLOCK, (j + 1) * COL_BLOCK)
        v_s[:, cols] = _dot(u_s[...], w_v_ref[:, cols]).astype(_BF16)
        o_s[:, cols] = _dot(u_s[...], w_o_ref[:, cols])
    gates = _dot(u_s[...], w_gate_ref[...]) + gate_b_ref[...]
    lane = lax.broadcasted_iota(jnp.int32, (T, LANES), 1)
    log_f = jnp.minimum(gates, 0.0) - jnp.log1p(jnp.exp(-jnp.abs(gates)))
    z_s[...] = jnp.where(lane < N_HEADS, gates, log_f)
    for j in range(D_CV // COL_BLOCK):
        cols = slice(j * COL_BLOCK, (j + 1) * COL_BLOCK)
        a = _dot(u_s[...], w_glu_ref[:, cols])
        gt = _dot(u_s[...], w_glu_ref[:, D_CV + j * COL_BLOCK:D_CV + (j + 1) * COL_BLOCK])
        store_lane_tiles(cv_buf, j, CV_HALO, a * _sigmoid(gt))

    off = QK_HALO - (QK_CONV - 1)
    for lt in range(2 * D_ML // LANES):
        cols = slice(lt * LANES, (lt + 1) * LANES)
        acc = qkc_b_ref[:, cols] + qkc_w_ref[0:1, cols] * qk_buf[lt, off:off + T, :]
        for k in range(1, QK_CONV):
            acc = acc + qkc_w_ref[k:k + 1, cols] * qk_buf[lt, off + k:off + k + T, :]
        act = acc * _sigmoid(acc)
        if lt < D_ML // LANES:
            q_s[:, cols] = act.astype(_BF16)
        else:
            k_s[:, lt * LANES - D_ML:(lt + 1) * LANES - D_ML] = act * (HEAD_DIM ** -0.5)
    qk_buf[:, 0:QK_HALO, :] = qk_buf[:, T:T + QK_HALO, :]

    cv_off = CV_HALO - (CV_KERNEL - 1)

    def cv_lane_tile(j, carry):
        def cv_rows(i, carry2):
            r0 = pl.multiple_of(i * CV_ROWS, CV_ROWS)
            acc = cv_b_ref[j] + cv_w_ref[j, 0:1, :] * cv_buf[j, pl.ds(r0 + cv_off, CV_ROWS), :]
            for k in range(1, CV_KERNEL):
                acc = acc + (cv_w_ref[j, k:k + 1, :]
                             * cv_buf[j, pl.ds(r0 + cv_off + k, CV_ROWS), :])
            y_s[j, pl.ds(r0, CV_ROWS), :] = acc
            return carry2

        lax.fori_loop(0, T // CV_ROWS, cv_rows, 0)
        return carry

    lax.fori_loop(0, D_CV // LANES, cv_lane_tile, 0)
    cv_buf[:, 0:CV_HALO, :] = cv_buf[:, T:T + CV_HALO, :]

    n_lt = D_CV // LANES
    tot = y_s[0]
    for j in range(1, n_lt):
        tot = tot + y_s[j]
    mu = jnp.sum(tot, axis=-1, keepdims=True) * (1.0 / D_CV)
    sq = jnp.square(y_s[0] - mu)
    for j in range(1, n_lt):
        sq = sq + jnp.square(y_s[j] - mu)
    var = jnp.sum(sq, axis=-1, keepdims=True) * (1.0 / D_CV)
    inv = lax.rsqrt(var + EPS)
    for j in range(n_lt):
        lcols = slice(j * LANES, (j + 1) * LANES)
        yn = (y_s[j] - mu) * inv * ln_g_ref[:, lcols] + ln_b_ref[:, lcols]
        hy_s[:, D_ML + j * LANES:D_ML + (j + 1) * LANES] = (yn * _sigmoid(yn)).astype(_BF16)

    L = ML_CHUNK
    row_i = lax.broadcasted_iota(jnp.int32, (L, L), 0)
    col_i = lax.broadcasted_iota(jnp.int32, (L, L), 1)
    causal = row_i >= col_i
    tril = causal.astype(_F32)
    lane_l = lax.broadcasted_iota(jnp.int32, (L, LANES), 1)

    def chunk_step(c, carry):
        r0 = pl.multiple_of(c * L, L)
        rows = pl.ds(r0, L)
        z = z_s[rows, :]
        csum = jnp.dot(tril, z, precision=lax.Precision.HIGHEST,
                       preferred_element_type=_F32)
        zb = jnp.where(lane_l < N_HEADS, z, csum)
        zb_t = zb.T
        for h in range(N_HEADS):
            hc = slice(h * HEAD_DIM, (h + 1) * HEAD_DIM)
            i_col = zb[:, h:h + 1]
            b_col = zb[:, N_HEADS + h:N_HEADS + h + 1]
            i_row = zb_t[h:h + 1, :]
            b_row = zb_t[N_HEADS + h:N_HEADS + h + 1, :]
            m_prev = m_s[h, 0:1, 0:1]
            q = q_s[rows, hc]
            k = k_s[rows, hc]
            v = v_s[rows, hc]
            dmat = jnp.where(causal, b_col - b_row + i_row, _NEG_INF)
            m_t = jnp.maximum(b_col + m_prev, jnp.max(dmat, axis=-1, keepdims=True))
            w_inter = jnp.exp(b_col + m_prev - m_t)
            s = lax.dot_general(q, k.astype(_BF16), (((1,), (1,)), ((), ())),
                                preferred_element_type=_F32) * jnp.exp(dmat - m_t)
            c_prev = c_s[h]
            num = w_inter * _dot(q, c_prev.astype(_BF16)) + _dot(s.astype(_BF16), v)
            qn = jnp.sum(q.astype(_F32) * n_s[h:h + 1, :], axis=-1, keepdims=True)
            den = w_inter * qn + jnp.sum(s, axis=-1, keepdims=True)
            hh = num * (1.0 / jnp.maximum(jnp.abs(den), jnp.exp(-m_t)))
            hn = hh * _rms_scale(hh) * mln_g_ref[:, hc]
            hy_s[rows, hc] = (hn * _sigmoid(o_s[rows, hc])).astype(_BF16)
            b_last = b_col[L - 1:L, :]
            g = b_last - b_col + i_col
            m_new = jnp.maximum(b_last + m_prev, jnp.max(g, axis=0, keepdims=True))
            decay = jnp.exp(b_last + m_prev - m_new)
            kw = k * jnp.exp(g - m_new)
            c_s[h] = decay * c_prev + lax.dot_general(
                kw.astype(_BF16), v, (((0,), (0,)), ((), ())), preferred_element_type=_F32)
            n_s[h:h + 1, :] = decay * n_s[h:h + 1, :] + jnp.sum(kw, axis=0, keepdims=True)
            m_s[h] = jnp.broadcast_to(m_new, (SUBLANES, LANES))
        return carry

    lax.fori_loop(0, T // L, chunk_step, 0)

    mix = _dot(hy_s[...], w_out_ref[...])
    out_ref[0] = x + g1 * (mix * _rms_scale(mix) * post_g_ref[...])


def _const_spec(shape):
    nd = len(shape)
    return pl.BlockSpec(shape, lambda b, t: (0,) * nd, pipeline_mode=pl.Buffered(1))


def _mixer(x, mod, pre_g, post_g, w_in, qkc_w, qkc_b, igate_b, fgate_b, mln_g,
           cv_w, cv_b, ln_g, ln_b, w_out):
    bsz, seq, d = x.shape
    T = SEQ_TILE
    w_qk = w_in[:, 0:2 * D_ML].astype(_BF16)
    w_v = w_in[:, 2 * D_ML:3 * D_ML].astype(_BF16)
    w_o = w_in[:, 3 * D_ML:4 * D_ML].astype(_BF16)
    n_gate = 2 * N_HEADS
    w_gate = jnp.pad(w_in[:, 4 * D_ML:4 * D_ML + n_gate],
                     ((0, 0), (0, LANES - n_gate))).astype(_BF16)
    gate_b = jnp.pad(jnp.concatenate([igate_b, fgate_b]), (0, LANES - n_gate)).reshape(1, LANES)
    w_glu = w_in[:, 4 * D_ML + n_gate:].astype(_BF16)
    n_lt = D_CV // LANES
    cv_w3 = jnp.pad(cv_w, ((0, 4 * SUBLANES - CV_KERNEL), (0, 0)))
    cv_w3 = cv_w3.reshape(4 * SUBLANES, n_lt, LANES).transpose(1, 0, 2)
    cv_b3 = cv_b.reshape(n_lt, 1, LANES)
    qkc_w8 = jnp.pad(qkc_w, ((0, SUBLANES - QK_CONV), (0, 0)))

    operands = [
        (x, pl.BlockSpec((1, T, d), lambda b, t: (b, t, 0))),
        (mod, pl.BlockSpec((1, 1, N_MOD * d), lambda b, t: (b, 0, 0))),
        (pre_g.reshape(1, d), None),
        (post_g.reshape(1, d), None),
        (w_qk, None), (w_v, None), (w_o, None), (w_gate, None), (gate_b, None), (w_glu, None),
        (qkc_w8, None), (qkc_b.reshape(1, 2 * D_ML), None), (mln_g.reshape(1, D_ML), None),
        (cv_w3, None), (cv_b3, None), (ln_g.reshape(1, D_CV), None), (ln_b.reshape(1, D_CV), None),
        (w_out.astype(_BF16), None),
    ]
    args = [a for a, _ in operands]
    in_specs = [s if s is not None else _const_spec(a.shape) for a, s in operands]
    scratch = [
        pltpu.VMEM((T, d), _BF16),
        pltpu.VMEM((2 * D_ML // LANES, QK_HALO + T, LANES), _F32),
        pltpu.VMEM((T, D_ML), _BF16),
        pltpu.VMEM((T, D_ML), _F32),
        pltpu.VMEM((T, D_ML), _BF16),
        pltpu.VMEM((T, D_ML), _F32),
        pltpu.VMEM((T, LANES), _F32),
        pltpu.VMEM((n_lt, CV_HALO + T, LANES), _F32),
        pltpu.VMEM((n_lt, T, LANES), _F32),
        pltpu.VMEM((T, D_ML + D_CV), _BF16),
        pltpu.VMEM((N_HEADS, HEAD_DIM, HEAD_DIM), _F32),
        pltpu.VMEM((SUBLANES, HEAD_DIM), _F32),
        pltpu.VMEM((N_HEADS, SUBLANES, LANES), _F32),
    ]
    return pl.pallas_call(
        _mixer_kernel,
        grid=(bsz, seq // T),
        in_specs=in_specs,
        out_specs=pl.BlockSpec((1, T, d), lambda b, t: (b, t, 0)),
        out_shape=jax.ShapeDtypeStruct(x.shape, x.dtype),
        scratch_shapes=scratch,
        compiler_params=pltpu.CompilerParams(
            dimension_semantics=("arbitrary", "arbitrary"),
            vmem_limit_bytes=VMEM_LIMIT_BYTES),
        name="mixer",
    )(*args)


def _ffn_kernel(x_ref, mod_ref, pre_g_ref, post_g_ref, up_ref, cw_ref, cb_ref, down_ref,
                out_ref, u_s, buf_s, halo_s, f_s):
    T = SEQ_TILE
    D = D_MODEL
    W = FFN_COL_BLOCK
    t_idx = pl.program_id(1)

    @pl.when(t_idx == 0)
    def _():
        halo_s[...] = jnp.zeros(halo_s.shape, _F32)

    x = x_ref[0]
    mod = mod_ref[0]
    sh2 = mod[:, 3 * D:4 * D]
    sc2 = mod[:, 4 * D:5 * D]
    g2 = mod[:, 5 * D:6 * D]
    u = (x * _rms_scale(x)) * pre_g_ref[...] * (1.0 + sc2) + sh2
    u_s[...] = u.astype(_BF16)

    off = QK_HALO - (FFN_KERNEL - 1)
    n_lt = W // LANES

    def up_conv(slot, part, col0):
        res = _dot(u_s[...], up_ref[:, col0:col0 + W])
        out = []
        for jj in range(n_lt):
            lt = col0 // LANES + jj
            cols = slice(lt * LANES, (lt + 1) * LANES)
            buf = buf_s.at[slot, part, jj]
            buf[0:QK_HALO, :] = halo_s[lt]
            buf[QK_HALO:QK_HALO + T, :] = res[:, jj * LANES:(jj + 1) * LANES]
            halo_s[lt] = buf[T:T + QK_HALO, :]
            acc = cb_ref[:, cols] + cw_ref[0:1, cols] * buf[off:off + T, :]
            for k in range(1, FFN_KERNEL):
                acc = acc + cw_ref[k:k + 1, cols] * buf[off + k:off + k + T, :]
            out.append(acc)
        return out

    for j in range(D_FF // W):
        a = up_conv(j % 2, 0, j * W)
        gt = up_conv(j % 2, 1, D_FF + j * W)
        for jj in range(n_lt):
            f_s[:, j * W + jj * LANES:j * W + (jj + 1) * LANES] = (
                gt[jj] * _sigmoid(gt[jj]) * a[jj]).astype(_BF16)

    f = _dot(f_s[...], down_ref[...])
    out_ref[0] = x + g2 * (f * _rms_scale(f) * post_g_ref[...])


def _ffn(x, mod, pre_g, post_g, up, conv_w, conv_b, down):
    bsz, seq, d = x.shape
    T = SEQ_TILE
    cw8 = jnp.pad(conv_w, ((0, SUBLANES - FFN_KERNEL), (0, 0)))
    operands = [
        (x, pl.BlockSpec((1, T, d), lambda b, t: (b, t, 0))),
        (mod, pl.BlockSpec((1, 1, N_MOD * d), lambda b, t: (b, 0, 0))),
        (pre_g.reshape(1, d), None),
        (post_g.reshape(1, d), None),
        (up.astype(_BF16), None),
        (cw8, None),
        (conv_b.reshape(1, 2 * D_FF), None),
        (down.astype(_BF16), None),
    ]
    args = [a for a, _ in operands]
    in_specs = [s if s is not None else _const_spec(a.shape) for a, s in operands]
    scratch = [
        pltpu.VMEM((T, d), _BF16),
        pltpu.VMEM((2, 2, FFN_COL_BLOCK // LANES, QK_HALO + T, LANES), _F32),
        pltpu.VMEM((2 * D_FF // LANES, QK_HALO, LANES), _F32),
        pltpu.VMEM((T, D_FF), _BF16),
    ]
    return pl.pallas_call(
        _ffn_kernel,
        grid=(bsz, seq // T),
        in_specs=in_specs,
        out_specs=pl.BlockSpec((1, T, d), lambda b, t: (b, t, 0)),
        out_shape=jax.ShapeDtypeStruct(x.shape, x.dtype),
        scratch_shapes=scratch,
        compiler_params=pltpu.CompilerParams(
            dimension_semantics=("arbitrary", "arbitrary"),
            vmem_limit_bytes=VMEM_LIMIT_BYTES),
        name="ffn",
    )(*args)


def kernel(x, c, ada_w, ada_b, mix_pre_g, mix_post_g, w_in, qk_conv_w, qk_conv_b, igate_b, fgate_b, ml_norm_g, cv_dw_w, cv_dw_b, cv_ln_g, cv_ln_b, w_out, ffn_pre_g, ffn_post_g, ffn_up, ffn_conv_w, ffn_conv_b, ffn_down):
    bsz, seq, d = x.shape
    depth = ada_w.shape[0]
    assert d == D_MODEL and seq % SEQ_TILE == 0
    mod_all = _modulation(c, ada_w, ada_b).reshape(depth, bsz, 1, N_MOD * d)
    for l in range(depth):
        mod = mod_all[l]
        x = _mixer(x, mod, mix_pre_g[l], mix_post_g[l], w_in[l], qk_conv_w[l], qk_conv_b[l],
                   igate_b[l], fgate_b[l], ml_norm_g[l], cv_dw_w[l], cv_dw_b[l],
                   cv_ln_g[l], cv_ln_b[l], w_out[l])
        x = _ffn(x, mod, ffn_pre_g[l], ffn_post_g[l], ffn_up[l], ffn_conv_w[l],
                 ffn_conv_b[l], ffn_down[l])
    return x
```

```python
import jax
import jax.numpy as jnp
from jax import lax
from jax.experimental import pallas as pl
from jax.experimental.pallas import tpu as pltpu

D_MODEL = 1024
N_HEADS = 4
HEAD_DIM = 256
D_ML = N_HEADS * HEAD_DIM
D_CV = 1024
QK_CONV = 4
CV_KERNEL = 31
ML_CHUNK = 128
D_FF = 2816
FFN_KERNEL = 3
N_MOD = 6
EPS = 1e-6

LANES = 128
SUBLANES = 8
SEQ_TILE = 512
QK_HALO = SUBLANES
CV_HALO = 4 * SUBLANES
CV_ROWS = 64
COL_BLOCK = 512
FFN_COL_BLOCK = 256
VMEM_LIMIT_BYTES = 58 * 1024 * 1024

LT_PER_BLOCK = COL_BLOCK // LANES
LT_PER_HEAD = HEAD_DIM // LANES

_F32 = jnp.float32
_BF16 = jnp.bfloat16
_NEG_INF = float("-inf")


def _dot(a, b):
    return jnp.dot(a, b, preferred_element_type=_F32)


def _sigmoid(x):
    return 0.5 * jnp.tanh(0.5 * x) + 0.5


def _rms_scale(x):
    return lax.rsqrt(jnp.mean(x * x, axis=-1, keepdims=True) + EPS)


def _mod_kernel(c_ref, w_ref, b_ref, o_ref):
    c = c_ref[...]
    cond = (c * _sigmoid(c)).astype(_BF16)
    o_ref[0] = _dot(cond, w_ref[0].astype(_BF16)) + b_ref[0]


def _modulation(c, ada_w, ada_b):
    depth, d, n = ada_w.shape
    bsz = c.shape[0]
    nb = n // d
    return pl.pallas_call(
        _mod_kernel,
        grid=(depth, nb),
        in_specs=[
            pl.BlockSpec((bsz, d), lambda l, j: (0, 0)),
            pl.BlockSpec((1, d, d), lambda l, j: (l, 0, j)),
            pl.BlockSpec((1, 1, d), lambda l, j: (l, 0, j)),
        ],
        out_specs=pl.BlockSpec((1, bsz, d), lambda l, j: (l, 0, j)),
        out_shape=jax.ShapeDtypeStruct((depth, bsz, n), _F32),
        compiler_params=pltpu.CompilerParams(
            dimension_semantics=("arbitrary", "arbitrary")),
        name="modulation",
    )(c, ada_w, ada_b.reshape(depth, 1, n))


def _mixer_kernel(x_ref, mod_ref, pre_g_ref, post_g_ref, w_head_ref, w_gate_ref, gate_b_ref,
                  w_glu_ref, qkc_w_ref, qkc_b_ref, mln_g_ref, cv_w_ref, cv_b_ref, ln_g_ref,
                  ln_b_ref, w_out_ref, out_ref,
                  u_s, proj_buf, q_s, kt_s, cv_buf, y_s, hy_s, cn_s, m_s):
    T = SEQ_TILE
    D = D_MODEL
    L = ML_CHUNK
    t_idx = pl.program_id(1)

    @pl.when(t_idx == 0)
    def _():
        proj_buf[:, 0:QK_HALO, :] = jnp.zeros((proj_buf.shape[0], QK_HALO, LANES), _F32)
        cv_buf[:, 0:CV_HALO, :] = jnp.zeros((D_CV // LANES, CV_HALO, LANES), _F32)
        cn_s[...] = jnp.zeros(cn_s.shape, _F32)
        m_s[...] = jnp.zeros(m_s.shape, _F32)

    x = x_ref[0]
    mod = mod_ref[0]
    sh1 = mod[:, 0:D]
    sc1 = mod[:, D:2 * D]
    g1 = mod[:, 2 * D:3 * D]
    u = (x * _rms_scale(x)) * pre_g_ref[...] * (1.0 + sc1) + sh1
    u_s[...] = u.astype(_BF16)

    for j in range(D_CV // COL_BLOCK):
        a = _dot(u_s[...], w_glu_ref[:, j * COL_BLOCK:(j + 1) * COL_BLOCK])
        gt = _dot(u_s[...], w_glu_ref[:, D_CV + j * COL_BLOCK:D_CV + (j + 1) * COL_BLOCK])
        glu = a * _sigmoid(gt)
        for jj in range(LT_PER_BLOCK):
            cv_buf[j * LT_PER_BLOCK + jj, CV_HALO:CV_HALO + T, :] = (
                glu[:, jj * LANES:(jj + 1) * LANES])

    gates = _dot(u_s[...], w_gate_ref[...]) + gate_b_ref[...]
    lane_t = lax.broadcasted_iota(jnp.int32, (T, LANES), 1)
    log_f = jnp.minimum(gates, 0.0) - jnp.log1p(jnp.exp(-jnp.abs(gates)))
    z = jnp.where(lane_t < N_HEADS, gates, log_f)
    row_i = lax.broadcasted_iota(jnp.int32, (L, L), 0)
    col_i = lax.broadcasted_iota(jnp.int32, (L, L), 1)
    causal = row_i >= col_i
    tril = causal.astype(_F32)
    lane_l = lax.broadcasted_iota(jnp.int32, (L, LANES), 1)
    zb_chunks, zbt_chunks = [], []
    for c in range(T // L):
        zc = z[c * L:(c + 1) * L, :]
        csum = jnp.dot(tril, zc, precision=lax.Precision.HIGHEST, preferred_element_type=_F32)
        zb = jnp.where(lane_l < N_HEADS, zc, csum)
        zb_chunks.append(zb)
        zbt_chunks.append(zb.T)

    cv_off = CV_HALO - (CV_KERNEL - 1)

    def cv_lane_tile(j, carry):
        def cv_rows(i, carry2):
            r0 = pl.multiple_of(i * CV_ROWS, CV_ROWS)
            acc = cv_b_ref[j] + cv_w_ref[j, 0:1, :] * cv_buf[j, pl.ds(r0 + cv_off, CV_ROWS), :]
            for k in range(1, CV_KERNEL):
                acc = acc + (cv_w_ref[j, k:k + 1, :]
                             * cv_buf[j, pl.ds(r0 + cv_off + k, CV_ROWS), :])
            y_s[j, pl.ds(r0, CV_ROWS), :] = acc
            return carry2

        lax.fori_loop(0, T // CV_ROWS, cv_rows, 0)
        return carry

    lax.fori_loop(0, D_CV // LANES, cv_lane_tile, 0)
    cv_buf[:, 0:CV_HALO, :] = cv_buf[:, T:T + CV_HALO, :]

    n_lt = D_CV // LANES
    tot = y_s[0]
    for j in range(1, n_lt):
        tot = tot + y_s[j]
    mu = jnp.sum(tot, axis=-1, keepdims=True) * (1.0 / D_CV)
    sq = jnp.square(y_s[0] - mu)
    for j in range(1, n_lt):
        sq = sq + jnp.square(y_s[j] - mu)
    var = jnp.sum(sq, axis=-1, keepdims=True) * (1.0 / D_CV)
    inv = lax.rsqrt(var + EPS)
    for j in range(n_lt):
        lcols = slice(j * LANES, (j + 1) * LANES)
        yn = (y_s[j] - mu) * inv * ln_g_ref[:, lcols] + ln_b_ref[:, lcols]
        hy_s[:, D_ML + j * LANES:D_ML + (j + 1) * LANES] = (yn * _sigmoid(yn)).astype(_BF16)

    off = QK_HALO - (QK_CONV - 1)

    def qk_conv(slab, col0):
        cols = slice(col0, col0 + LANES)
        acc = qkc_b_ref[:, cols] + qkc_w_ref[0:1, cols] * proj_buf[slab, off:off + T, :]
        for k in range(1, QK_CONV):
            acc = acc + qkc_w_ref[k:k + 1, cols] * proj_buf[slab, off + k:off + k + T, :]
        return acc * _sigmoid(acc)

    ones_blk = jnp.ones((L, LANES), _BF16)
    mix = _dot(hy_s[:, D_ML:D_ML + D_CV], w_out_ref[D_ML:D_ML + D_CV, :])
    for h in range(N_HEADS):
        hc = slice(h * HEAD_DIM, (h + 1) * HEAD_DIM)
        slab_q = h * 4 * LT_PER_HEAD
        slab_k = slab_q + LT_PER_HEAD
        slab_v = slab_k + LT_PER_HEAD
        slab_o = slab_v + LT_PER_HEAD
        res = _dot(u_s[...], w_head_ref[h])
        for jj in range(4 * LT_PER_HEAD):
            proj_buf[slab_q + jj, QK_HALO:QK_HALO + T, :] = res[:, jj * LANES:(jj + 1) * LANES]
        for jj in range(LT_PER_HEAD):
            lt = h * LT_PER_HEAD + jj
            q_s[:, lt * LANES:(lt + 1) * LANES] = qk_conv(slab_q + jj, lt * LANES).astype(_BF16)
            kt_s[lt * LANES:(lt + 1) * LANES, :] = (
                qk_conv(slab_k + jj, D_ML + lt * LANES) * (HEAD_DIM ** -0.5)).T
        proj_buf[slab_q:slab_v, 0:QK_HALO, :] = proj_buf[slab_q:slab_v, T:T + QK_HALO, :]

        cn = cn_s[h]
        m_prev = m_s[h, 0:1, :]
        for c in range(T // L):
            r = slice(c * L, (c + 1) * L)
            pr = slice(QK_HALO + c * L, QK_HALO + (c + 1) * L)
            zb, zbt = zb_chunks[c], zbt_chunks[c]
            b_rep = jnp.broadcast_to(zb[:, N_HEADS + h:N_HEADS + h + 1], (L, LANES))
            i_row = zbt[h:h + 1, :]
            b_row = zbt[N_HEADS + h:N_HEADS + h + 1, :]
            q = q_s[r, hc]
            kt = kt_s[hc, r]
            v_aug = jnp.concatenate(
                [proj_buf[slab_v + jj, pr, :].astype(_BF16)
                 for jj in range(LT_PER_HEAD)] + [ones_blk], axis=1)
            dmat = jnp.where(causal, b_rep + (i_row - b_row), _NEG_INF)
            a_rep = jnp.broadcast_to(jnp.max(dmat, axis=-1, keepdims=True), (L, LANES))
            s = _dot(q, kt.astype(_BF16)) * jnp.exp(dmat - a_rep)
            intra = _dot(s.astype(_BF16), v_aug)
            inter = _dot(q, cn.astype(_BF16))
            bm = b_rep + m_prev
            m_t = jnp.maximum(bm, a_rep)
            w_inter = jnp.exp(bm - m_t)
            w_intra = jnp.exp(a_rep - m_t)
            comb = [w_inter * inter[:, jj * LANES:(jj + 1) * LANES]
                    + w_intra * intra[:, jj * LANES:(jj + 1) * LANES]
                    for jj in range(LT_PER_HEAD + 1)]
            den = comb[LT_PER_HEAD]
            inv_den = 1.0 / jnp.maximum(jnp.abs(den), jnp.exp(-m_t))
            hh = [comb[jj] * inv_den for jj in range(LT_PER_HEAD)]
            sq = hh[0] * hh[0]
            for jj in range(1, LT_PER_HEAD):
                sq = sq + hh[jj] * hh[jj]
            rs = jnp.broadcast_to(
                lax.rsqrt(jnp.sum(sq, axis=-1, keepdims=True) * (1.0 / HEAD_DIM) + EPS), (L, LANES))
            for jj in range(LT_PER_HEAD):
                cols = slice(h * HEAD_DIM + jj * LANES, h * HEAD_DIM + (jj + 1) * LANES)
                o_pre = proj_buf[slab_o + jj, pr, :]
                hy_s[r, cols] = (hh[jj] * rs * mln_g_ref[:, cols] * _sigmoid(o_pre)).astype(_BF16)
            b_last = b_rep[L - 1:L, :]
            g_row = b_last - b_row + i_row
            mg = jnp.broadcast_to(jnp.max(g_row, axis=-1, keepdims=True), (1, LANES))
            m_new = jnp.maximum(b_last + m_prev, mg)
            kwt = kt * jnp.exp(g_row - mg)
            upd = _dot(kwt.astype(_BF16), v_aug)
            decay = jnp.exp(b_last + m_prev - m_new)
            gain = jnp.exp(mg - m_new)
            cn = jnp.concatenate(
                [decay * cn[:, jj * LANES:(jj + 1) * LANES] + gain * upd[:, jj * LANES:(jj + 1) * LANES]
                 for jj in range(LT_PER_HEAD + 1)], axis=1)
            m_prev = m_new
        cn_s[h] = cn
        m_s[h] = jnp.broadcast_to(m_prev, (SUBLANES, LANES))
        mix = mix + _dot(hy_s[:, hc], w_out_ref[hc, :])

    out_ref[0] = x + g1 * (mix * _rms_scale(mix) * post_g_ref[...])


def _const_spec(shape):
    nd = len(shape)
    return pl.BlockSpec(shape, lambda b, t: (0,) * nd, pipeline_mode=pl.Buffered(1))


def _mixer(x, mod, pre_g, post_g, w_in, qkc_w, qkc_b, igate_b, fgate_b, mln_g,
           cv_w, cv_b, ln_g, ln_b, w_out):
    bsz, seq, d = x.shape
    T = SEQ_TILE
    n_lt = D_CV // LANES
    w_head = w_in[:, 0:4 * D_ML].astype(_BF16).reshape(d, 4, N_HEADS, HEAD_DIM)
    w_head = w_head.transpose(2, 0, 1, 3).reshape(N_HEADS, d, 4 * HEAD_DIM)
    n_gate = 2 * N_HEADS
    w_gate = jnp.pad(w_in[:, 4 * D_ML:4 * D_ML + n_gate],
                     ((0, 0), (0, LANES - n_gate))).astype(_BF16)
    gate_b = jnp.pad(jnp.concatenate([igate_b, fgate_b]), (0, LANES - n_gate)).reshape(1, LANES)
    w_glu = w_in[:, 4 * D_ML + n_gate:].astype(_BF16)
    cv_w3 = jnp.pad(cv_w, ((0, 4 * SUBLANES - CV_KERNEL), (0, 0)))
    cv_w3 = cv_w3.reshape(4 * SUBLANES, n_lt, LANES).transpose(1, 0, 2)
    cv_b3 = cv_b.reshape(n_lt, 1, LANES)
    qkc_w8 = jnp.pad(qkc_w, ((0, SUBLANES - QK_CONV), (0, 0)))

    operands = [
        (x, pl.BlockSpec((1, T, d), lambda b, t: (b, t, 0))),
        (mod, pl.BlockSpec((1, 1, N_MOD * d), lambda b, t: (b, 0, 0))),
        (pre_g.reshape(1, d), None),
        (post_g.reshape(1, d), None),
        (w_head, None), (w_gate, None), (gate_b, None), (w_glu, None),
        (qkc_w8, None), (qkc_b.reshape(1, 2 * D_ML), None), (mln_g.reshape(1, D_ML), None),
        (cv_w3, None), (cv_b3, None), (ln_g.reshape(1, D_CV), None), (ln_b.reshape(1, D_CV), None),
        (w_out.astype(_BF16), None),
    ]
    args = [a for a, _ in operands]
    in_specs = [s if s is not None else _const_spec(a.shape) for a, s in operands]
    scratch = [
        pltpu.VMEM((T, d), _BF16),
        pltpu.VMEM((4 * D_ML // LANES, QK_HALO + T, LANES), _F32),
        pltpu.VMEM((T, D_ML), _BF16),
        pltpu.VMEM((D_ML, T), _F32),
        pltpu.VMEM((n_lt, CV_HALO + T, LANES), _F32),
        pltpu.VMEM((n_lt, T, LANES), _F32),
        pltpu.VMEM((T, D_ML + D_CV), _BF16),
        pltpu.VMEM((N_HEADS, HEAD_DIM, HEAD_DIM + LANES), _F32),
        pltpu.VMEM((N_HEADS, SUBLANES, LANES), _F32),
    ]
    return pl.pallas_call(
        _mixer_kernel,
        grid=(bsz, seq // T),
        in_specs=in_specs,
        out_specs=pl.BlockSpec((1, T, d), lambda b, t: (b, t, 0)),
        out_shape=jax.ShapeDtypeStruct(x.shape, x.dtype),
        scratch_shapes=scratch,
        compiler_params=pltpu.CompilerParams(
            dimension_semantics=("arbitrary", "arbitrary"),
            vmem_limit_bytes=VMEM_LIMIT_BYTES),
        name="mixer",
    )(*args)


def _ffn_kernel(x_ref, mod_ref, pre_g_ref, post_g_ref, up_ref, cw_ref, cb_ref, down_ref,
                out_ref, u_s, buf_s, halo_s, f_s):
    T = SEQ_TILE
    D = D_MODEL
    W = FFN_COL_BLOCK
    t_idx = pl.program_id(1)

    @pl.when(t_idx == 0)
    def _():
        halo_s[...] = jnp.zeros(halo_s.shape, _F32)

    x = x_ref[0]
    mod = mod_ref[0]
    sh2 = mod[:, 3 * D:4 * D]
    sc2 = mod[:, 4 * D:5 * D]
    g2 = mod[:, 5 * D:6 * D]
    u = (x * _rms_scale(x)) * pre_g_ref[...] * (1.0 + sc2) + sh2
    u_s[...] = u.astype(_BF16)

    off = QK_HALO - (FFN_KERNEL - 1)
    n_lt = W // LANES

    def up_conv(slot, part, col0):
        res = _dot(u_s[...], up_ref[:, col0:col0 + W])
        out = []
        for jj in range(n_lt):
            lt = col0 // LANES + jj
            cols = slice(lt * LANES, (lt + 1) * LANES)
            buf = buf_s.at[slot, part, jj]
            buf[0:QK_HALO, :] = halo_s[lt]
            buf[QK_HALO:QK_HALO + T, :] = res[:, jj * LANES:(jj + 1) * LANES]
            halo_s[lt] = buf[T:T + QK_HALO, :]
            acc = cb_ref[:, cols] + cw_ref[0:1, cols] * buf[off:off + T, :]
            for k in range(1, FFN_KERNEL):
                acc = acc + cw_ref[k:k + 1, cols] * buf[off + k:off + k + T, :]
            out.append(acc)
        return out

    for j in range(D_FF // W):
        a = up_conv(j % 2, 0, j * W)
        gt = up_conv(j % 2, 1, D_FF + j * W)
        for jj in range(n_lt):
            f_s[:, j * W + jj * LANES:j * W + (jj + 1) * LANES] = (
                gt[jj] * _sigmoid(gt[jj]) * a[jj]).astype(_BF16)

    f = _dot(f_s[...], down_ref[...])
    out_ref[0] = x + g2 * (f * _rms_scale(f) * post_g_ref[...])


def _ffn(x, mod, pre_g, post_g, up, conv_w, conv_b, down):
    bsz, seq, d = x.shape
    T = SEQ_TILE
    cw8 = jnp.pad(conv_w, ((0, SUBLANES - FFN_KERNEL), (0, 0)))
    operands = [
        (x, pl.BlockSpec((1, T, d), lambda b, t: (b, t, 0))),
        (mod, pl.BlockSpec((1, 1, N_MOD * d), lambda b, t: (b, 0, 0))),
        (pre_g.reshape(1, d), None),
        (post_g.reshape(1, d), None),
        (up.astype(_BF16), None),
        (cw8, None),
        (conv_b.reshape(1, 2 * D_FF), None),
        (down.astype(_BF16), None),
    ]
    args = [a for a, _ in operands]
    in_specs = [s if s is not None else _const_spec(a.shape) for a, s in operands]
    scratch = [
        pltpu.VMEM((T, d), _BF16),
        pltpu.VMEM((2, 2, FFN_COL_BLOCK // LANES, QK_HALO + T, LANES), _F32),
        pltpu.VMEM((2 * D_FF // LANES, QK_HALO, LANES), _F32),
        pltpu.VMEM((T, D_FF), _BF16),
    ]
    return pl.pallas_call(
        _ffn_kernel,
        grid=(bsz, seq // T),
        in_specs=in_specs,
        out_specs=pl.BlockSpec((1, T, d), lambda b, t: (b, t, 0)),
        out_shape=jax.ShapeDtypeStruct(x.shape, x.dtype),
        scratch_shapes=scratch,
        compiler_params=pltpu.CompilerParams(
            dimension_semantics=("arbitrary", "arbitrary"),
            vmem_limit_bytes=VMEM_LIMIT_BYTES),
        name="ffn",
    )(*args)


def kernel(x, c, ada_w, ada_b, mix_pre_g, mix_post_g, w_in, qk_conv_w, qk_conv_b, igate_b, fgate_b, ml_norm_g, cv_dw_w, cv_dw_b, cv_ln_g, cv_ln_b, w_out, ffn_pre_g, ffn_post_g, ffn_up, ffn_conv_w, ffn_conv_b, ffn_down):
    bsz, seq, d = x.shape
    depth = ada_w.shape[0]
    assert d == D_MODEL and seq % SEQ_TILE == 0
    mod_all = _modulation(c, ada_w, ada_b).reshape(depth, bsz, 1, N_MOD * d)
    for l in range(depth):
        mod = mod_all[l]
        x = _mixer(x, mod, mix_pre_g[l], mix_post_g[l], w_in[l], qk_conv_w[l], qk_conv_b[l],
                   igate_b[l], fgate_b[l], ml_norm_g[l], cv_dw_w[l], cv_dw_b[l],
                   cv_ln_g[l], cv_ln_b[l], w_out[l])
        x = _ffn(x, mod, ffn_pre_g[l], ffn_post_g[l], ffn_up[l], ffn_conv_w[l],
                 ffn_conv_b[l], ffn_down[l])
    return x
```

```python
import jax
import jax.numpy as jnp
from jax import lax
from jax.experimental import pallas as pl
from jax.experimental.pallas import tpu as pltpu

D_MODEL = 1024
N_HEADS = 4
HEAD_DIM = 256
D_ML = N_HEADS * HEAD_DIM
D_CV = 1024
QK_CONV = 4
CV_KERNEL = 31
ML_CHUNK = 128
D_FF = 2816
FFN_KERNEL = 3
N_MOD = 6
N_GATE = 2 * N_HEADS
EPS = 1e-6

LANES = 128
SUBLANES = 8
SEQ_TILE = 512
QK_HALO = SUBLANES
CV_HALO = 4 * SUBLANES
CV_ROWS = 64
FFN_COL_BLOCK = 256
VMEM_LIMIT_BYTES = 58 * 1024 * 1024

LT_PER_HEAD = HEAD_DIM // LANES
N_CV_LT = D_CV // LANES
GATE_COL = 4 * D_ML

_F32 = jnp.float32
_BF16 = jnp.bfloat16
_NEG_INF = float("-inf")


def _dot(a, b):
    return jnp.dot(a, b, preferred_element_type=_F32)


def _sigmoid(x):
    return 0.5 * jnp.tanh(0.5 * x) + 0.5


def _silu(x):
    h = 0.5 * x
    return h * jnp.tanh(h) + h


def _rms_scale(x):
    return lax.rsqrt(jnp.mean(x * x, axis=-1, keepdims=True) + EPS)


def _mod_kernel(c_ref, w_ref, b_ref, o_ref):
    cond = _silu(c_ref[...]).astype(_BF16)
    o_ref[0] = _dot(cond, w_ref[0].astype(_BF16)) + b_ref[0]


def _modulation(c, ada_w, ada_b):
    depth, d, n = ada_w.shape
    bsz = c.shape[0]
    nb = n // d
    return pl.pallas_call(
        _mod_kernel,
        grid=(depth, nb),
        in_specs=[
            pl.BlockSpec((bsz, d), lambda l, j: (0, 0)),
            pl.BlockSpec((1, d, d), lambda l, j: (l, 0, j)),
            pl.BlockSpec((1, 1, d), lambda l, j: (l, 0, j)),
        ],
        out_specs=pl.BlockSpec((1, bsz, d), lambda l, j: (l, 0, j)),
        out_shape=jax.ShapeDtypeStruct((depth, bsz, n), _F32),
        compiler_params=pltpu.CompilerParams(
            dimension_semantics=("arbitrary", "arbitrary")),
        name="modulation",
    )(c, ada_w, ada_b.reshape(depth, 1, n))


def _mixer_kernel(x_ref, mod_ref, pre_g_ref, post_g_ref, w_main_ref, gate_b_ref, w_glu_ref,
                  qkc_w_ref, qkc_b_ref, mln_g_ref, cv_w_ref, cv_b_ref, ln_g_ref, ln_b_ref,
                  w_out_ref, out_ref,
                  u_s, proj_buf, q_s, kt_s, cv_buf, y_s, hy_s, cn_s, m_s):
    T = SEQ_TILE
    D = D_MODEL
    L = ML_CHUNK
    t_idx = pl.program_id(1)

    @pl.when(t_idx == 0)
    def _():
        proj_buf[:, 0:QK_HALO, :] = jnp.zeros((proj_buf.shape[0], QK_HALO, LANES), _F32)
        cv_buf[:, 0:CV_HALO, :] = jnp.zeros((cv_buf.shape[0], CV_HALO, LANES), _F32)
        cn_s[...] = jnp.zeros(cn_s.shape, _F32)
        m_s[...] = jnp.zeros(m_s.shape, _F32)

    x = x_ref[0]
    mod = mod_ref[0, 0]
    sh1 = mod[:, 0:D]
    sc1 = mod[:, D:2 * D]
    g1 = mod[:, 2 * D:3 * D]
    u = (x * _rms_scale(x)) * pre_g_ref[...] * (1.0 + sc1) + sh1
    u_s[...] = u.astype(_BF16)

    def glu_block(rows, jb):
        ag = _dot(u_s[rows, :], w_glu_ref[0, jb])
        return ag[:, 0:LANES] * _sigmoid(ag[:, LANES:2 * LANES])

    for jb in range(N_CV_LT):
        cv_buf[jb, CV_HALO:CV_HALO + T, :] = glu_block(slice(0, T), jb)
    cv_off = CV_HALO - (CV_KERNEL - 1)

    def cv_lane_tile(j, carry):
        def cv_rows(i, carry2):
            r0 = pl.multiple_of(i * CV_ROWS, CV_ROWS)
            acc = cv_b_ref[j] + cv_w_ref[j, 0:1, :] * cv_buf[j, pl.ds(r0 + cv_off, CV_ROWS), :]
            for k in range(1, CV_KERNEL):
                acc = acc + (cv_w_ref[j, k:k + 1, :]
                             * cv_buf[j, pl.ds(r0 + cv_off + k, CV_ROWS), :])
            y_s[j, pl.ds(r0, CV_ROWS), :] = acc
            return carry2

        lax.fori_loop(0, T // CV_ROWS, cv_rows, 0)
        return carry

    lax.fori_loop(0, N_CV_LT, cv_lane_tile, 0)
    cv_buf[:, 0:CV_HALO, :] = cv_buf[:, T:T + CV_HALO, :]

    tot = y_s[0]
    for j in range(1, N_CV_LT):
        tot = tot + y_s[j]
    mu = jnp.broadcast_to(jnp.sum(tot, axis=-1, keepdims=True) * (1.0 / D_CV), (T, LANES))
    sq = jnp.square(y_s[0] - mu)
    for j in range(1, N_CV_LT):
        sq = sq + jnp.square(y_s[j] - mu)
    var = jnp.sum(sq, axis=-1, keepdims=True) * (1.0 / D_CV)
    inv = jnp.broadcast_to(lax.rsqrt(var + EPS), (T, LANES))
    for j in range(N_CV_LT):
        lcols = slice(j * LANES, (j + 1) * LANES)
        yn = (y_s[j] - mu) * inv * ln_g_ref[:, lcols] + ln_b_ref[:, lcols]
        hy_s[:, D_ML + j * LANES:D_ML + (j + 1) * LANES] = _silu(yn).astype(_BF16)

    gates = _dot(u_s[...], w_main_ref[0, :, GATE_COL:GATE_COL + LANES]) + gate_b_ref[...]
    lane_t = lax.broadcasted_iota(jnp.int32, (T, LANES), 1)
    log_f = jnp.minimum(gates, 0.0) - jnp.log1p(jnp.exp(-jnp.abs(gates)))
    z = jnp.where(lane_t < N_HEADS, gates, log_f)
    row_i = lax.broadcasted_iota(jnp.int32, (L, L), 0)
    col_i = lax.broadcasted_iota(jnp.int32, (L, L), 1)
    causal = row_i >= col_i
    tril = causal.astype(_F32)
    lane_l = lax.broadcasted_iota(jnp.int32, (L, LANES), 1)
    zb_chunks, zbt_chunks = [], []
    for c in range(T // L):
        zc = z[c * L:(c + 1) * L, :]
        csum = jnp.dot(tril, zc, precision=lax.Precision.HIGHEST, preferred_element_type=_F32)
        zb = jnp.where(lane_l < N_HEADS, zc, csum)
        zb_chunks.append(zb)
        zbt_chunks.append(zb.T)

    off = QK_HALO - (QK_CONV - 1)

    def qk_conv(slab, col0):
        cols = slice(col0, col0 + LANES)
        acc = qkc_b_ref[:, cols] + qkc_w_ref[0:1, cols] * proj_buf[slab, off:off + T, :]
        for k in range(1, QK_CONV):
            acc = acc + qkc_w_ref[k:k + 1, cols] * proj_buf[slab, off + k:off + k + T, :]
        return _silu(acc)

    ones_blk = jnp.ones((L, LANES), _BF16)
    mix = _dot(hy_s[:, D_ML:D_ML + D_CV], w_out_ref[0, D_ML:D_ML + D_CV, :])
    for h in range(N_HEADS):
        hc = slice(h * HEAD_DIM, (h + 1) * HEAD_DIM)
        slab_q = h * 4 * LT_PER_HEAD
        slab_k = slab_q + LT_PER_HEAD
        slab_v = slab_k + LT_PER_HEAD
        slab_o = slab_v + LT_PER_HEAD
        for g in range(4):
            c0 = g * D_ML + h * HEAD_DIM
            res = _dot(u_s[...], w_main_ref[0, :, c0:c0 + HEAD_DIM])
            for jj in range(LT_PER_HEAD):
                proj_buf[slab_q + g * LT_PER_HEAD + jj, QK_HALO:QK_HALO + T, :] = (
                    res[:, jj * LANES:(jj + 1) * LANES])
        for jj in range(LT_PER_HEAD):
            lt = h * LT_PER_HEAD + jj
            q_s[:, lt * LANES:(lt + 1) * LANES] = qk_conv(slab_q + jj, lt * LANES).astype(_BF16)
            kt_s[lt * LANES:(lt + 1) * LANES, :] = (
                qk_conv(slab_k + jj, D_ML + lt * LANES) * (HEAD_DIM ** -0.5)).T
        proj_buf[slab_q:slab_v, 0:QK_HALO, :] = proj_buf[slab_q:slab_v, T:T + QK_HALO, :]

        cn = cn_s[h]
        m_prev = m_s[h, 0:1, :]
        for c in range(T // L):
            r = slice(c * L, (c + 1) * L)
            pr = slice(QK_HALO + c * L, QK_HALO + (c + 1) * L)
            zb, zbt = zb_chunks[c], zbt_chunks[c]
            b_rep = jnp.broadcast_to(zb[:, N_HEADS + h:N_HEADS + h + 1], (L, LANES))
            i_row = zbt[h:h + 1, :]
            b_row = zbt[N_HEADS + h:N_HEADS + h + 1, :]
            q = q_s[r, hc]
            kt = kt_s[hc, r]
            v_aug = jnp.concatenate(
                [proj_buf[slab_v + jj, pr, :].astype(_BF16)
                 for jj in range(LT_PER_HEAD)] + [ones_blk], axis=1)
            dmat = jnp.where(causal, b_rep + (i_row - b_row), _NEG_INF)
            bm = b_rep + m_prev
            m_t = jnp.maximum(bm, jnp.broadcast_to(jnp.max(dmat, axis=-1, keepdims=True), (L, LANES)))
            s = _dot(q, kt.astype(_BF16)) * jnp.exp(dmat - m_t)
            intra = _dot(s.astype(_BF16), v_aug)
            inter = _dot(q, cn.astype(_BF16))
            w_inter = jnp.exp(bm - m_t)
            comb = [w_inter * inter[:, jj * LANES:(jj + 1) * LANES] + intra[:, jj * LANES:(jj + 1) * LANES]
                    for jj in range(LT_PER_HEAD + 1)]
            den = comb[LT_PER_HEAD]
            inv_den = 1.0 / jnp.maximum(jnp.abs(den), jnp.exp(-m_t))
            hh = [comb[jj] * inv_den for jj in range(LT_PER_HEAD)]
            hsq = hh[0] * hh[0]
            for jj in range(1, LT_PER_HEAD):
                hsq = hsq + hh[jj] * hh[jj]
            rs = jnp.broadcast_to(
                lax.rsqrt(jnp.sum(hsq, axis=-1, keepdims=True) * (1.0 / HEAD_DIM) + EPS), (L, LANES))
            for jj in range(LT_PER_HEAD):
                cols = slice(h * HEAD_DIM + jj * LANES, h * HEAD_DIM + (jj + 1) * LANES)
                o_pre = proj_buf[slab_o + jj, pr, :]
                hy_s[r, cols] = (hh[jj] * rs * mln_g_ref[:, cols] * _sigmoid(o_pre)).astype(_BF16)
            b_last = b_rep[L - 1:L, :]
            g_row = b_last - b_row + i_row
            m_new = jnp.maximum(
                b_last + m_prev,
                jnp.broadcast_to(jnp.max(g_row, axis=-1, keepdims=True), (1, LANES)))
            kwt = kt * jnp.exp(g_row - m_new)
            upd = _dot(kwt.astype(_BF16), v_aug)
            decay = jnp.exp(b_last + m_prev - m_new)
            cn = jnp.concatenate(
                [decay * cn[:, jj * LANES:(jj + 1) * LANES] + upd[:, jj * LANES:(jj + 1) * LANES]
                 for jj in range(LT_PER_HEAD + 1)], axis=1)
            m_prev = m_new
        cn_s[h] = cn
        m_s[h] = jnp.broadcast_to(m_prev, (SUBLANES, LANES))
        mix = mix + _dot(hy_s[:, hc], w_out_ref[0, hc, :])

    out_ref[0] = x + g1 * (mix * _rms_scale(mix) * post_g_ref[...])


def _const_spec(shape, index=None):
    index = index if index is not None else (0,) * len(shape)
    return pl.BlockSpec(shape, lambda b, t: index, pipeline_mode=pl.Buffered(1))


def _layer_spec(arr, l):
    return _const_spec((1,) + arr.shape[1:], (l,) + (0,) * (arr.ndim - 1))


def _pack_in_proj(w_in):
    depth, d, _ = w_in.shape
    w_main = jnp.pad(w_in[:, :, 0:GATE_COL + N_GATE].astype(_BF16),
                     ((0, 0), (0, 0), (0, LANES - N_GATE)))
    w_glu = w_in[:, :, GATE_COL + N_GATE:].astype(_BF16).reshape(depth, d, 2, N_CV_LT, LANES)
    w_glu = w_glu.transpose(0, 3, 1, 2, 4).reshape(depth, N_CV_LT, d, 2 * LANES)
    return w_main, w_glu


def _mixer(l, x, mod_all, pre_g, post_g, w_main, w_glu, qkc_w, qkc_b, igate_b, fgate_b, mln_g,
           cv_w, cv_b, ln_g, ln_b, w_out):
    bsz, seq, d = x.shape
    T = SEQ_TILE
    gate_b = jnp.pad(jnp.concatenate([igate_b, fgate_b]), (0, LANES - N_GATE)).reshape(1, LANES)
    cv_w3 = jnp.pad(cv_w, ((0, 4 * SUBLANES - CV_KERNEL), (0, 0)))
    cv_w3 = cv_w3.reshape(4 * SUBLANES, N_CV_LT, LANES).transpose(1, 0, 2)
    cv_b3 = cv_b.reshape(N_CV_LT, 1, LANES)
    qkc_w8 = jnp.pad(qkc_w, ((0, SUBLANES - QK_CONV), (0, 0)))

    operands = [
        (x, pl.BlockSpec((1, T, d), lambda b, t: (b, t, 0))),
        (mod_all, pl.BlockSpec((1, 1, 1, N_MOD * d), lambda b, t: (l, b, 0, 0))),
        (pre_g.reshape(1, d), None),
        (post_g.reshape(1, d), None),
        (w_main, _layer_spec(w_main, l)),
        (gate_b, None),
        (w_glu, _layer_spec(w_glu, l)),
        (qkc_w8, None), (qkc_b.reshape(1, 2 * D_ML), None), (mln_g.reshape(1, D_ML), None),
        (cv_w3, None), (cv_b3, None), (ln_g.reshape(1, D_CV), None), (ln_b.reshape(1, D_CV), None),
        (w_out, _layer_spec(w_out, l)),
    ]
    args = [a for a, _ in operands]
    in_specs = [s if s is not None else _const_spec(a.shape) for a, s in operands]
    scratch = [
        pltpu.VMEM((T, d), _BF16),
        pltpu.VMEM((4 * D_ML // LANES, QK_HALO + T, LANES), _F32),
        pltpu.VMEM((T, D_ML), _BF16),
        pltpu.VMEM((D_ML, T), _F32),
        pltpu.VMEM((N_CV_LT, CV_HALO + T, LANES), _F32),
        pltpu.VMEM((N_CV_LT, T, LANES), _F32),
        pltpu.VMEM((T, D_ML + D_CV), _BF16),
        pltpu.VMEM((N_HEADS, HEAD_DIM, HEAD_DIM + LANES), _F32),
        pltpu.VMEM((N_HEADS, SUBLANES, LANES), _F32),
    ]
    return pl.pallas_call(
        _mixer_kernel,
        grid=(bsz, seq // T),
        in_specs=in_specs,
        out_specs=pl.BlockSpec((1, T, d), lambda b, t: (b, t, 0)),
        out_shape=jax.ShapeDtypeStruct(x.shape, x.dtype),
        scratch_shapes=scratch,
        compiler_params=pltpu.CompilerParams(
            dimension_semantics=("arbitrary", "arbitrary"),
            vmem_limit_bytes=VMEM_LIMIT_BYTES),
        name="mixer",
    )(*args)


def _ffn_kernel(x_ref, mod_ref, pre_g_ref, post_g_ref, up_ref, cw_ref, cb_ref, down_ref,
                out_ref, u_s, buf_s, halo_s, f_s):
    T = SEQ_TILE
    D = D_MODEL
    W = FFN_COL_BLOCK
    t_idx = pl.program_id(1)

    @pl.when(t_idx == 0)
    def _():
        halo_s[...] = jnp.zeros(halo_s.shape, _F32)

    x = x_ref[0]
    mod = mod_ref[0, 0]
    sh2 = mod[:, 3 * D:4 * D]
    sc2 = mod[:, 4 * D:5 * D]
    g2 = mod[:, 5 * D:6 * D]
    u = (x * _rms_scale(x)) * pre_g_ref[...] * (1.0 + sc2) + sh2
    u_s[...] = u.astype(_BF16)

    off = QK_HALO - (FFN_KERNEL - 1)
    n_lt = W // LANES

    def up_conv(slot, part, col0):
        res = _dot(u_s[...], up_ref[0, :, col0:col0 + W])
        out = []
        for jj in range(n_lt):
            lt = col0 // LANES + jj
            cols = slice(lt * LANES, (lt + 1) * LANES)
            buf = buf_s.at[slot, part, jj]
            buf[0:QK_HALO, :] = halo_s[lt]
            buf[QK_HALO:QK_HALO + T, :] = res[:, jj * LANES:(jj + 1) * LANES]
            halo_s[lt] = buf[T:T + QK_HALO, :]
            acc = cb_ref[:, cols] + cw_ref[0:1, cols] * buf[off:off + T, :]
            for k in range(1, FFN_KERNEL):
                acc = acc + cw_ref[k:k + 1, cols] * buf[off + k:off + k + T, :]
            out.append(acc)
        return out

    for j in range(D_FF // W):
        a = up_conv(j % 2, 0, j * W)
        gt = up_conv(j % 2, 1, D_FF + j * W)
        for jj in range(n_lt):
            f_s[:, j * W + jj * LANES:j * W + (jj + 1) * LANES] = (
                _silu(gt[jj]) * a[jj]).astype(_BF16)

    f = _dot(f_s[...], down_ref[0])
    out_ref[0] = x + g2 * (f * _rms_scale(f) * post_g_ref[...])


def _ffn(l, x, mod_all, pre_g, post_g, up, conv_w, conv_b, down):
    bsz, seq, d = x.shape
    T = SEQ_TILE
    cw8 = jnp.pad(conv_w, ((0, SUBLANES - FFN_KERNEL), (0, 0)))
    operands = [
        (x, pl.BlockSpec((1, T, d), lambda b, t: (b, t, 0))),
        (mod_all, pl.BlockSpec((1, 1, 1, N_MOD * d), lambda b, t: (l, b, 0, 0))),
        (pre_g.reshape(1, d), None),
        (post_g.reshape(1, d), None),
        (up, _layer_spec(up, l)),
        (cw8, None),
        (conv_b.reshape(1, 2 * D_FF), None),
        (down, _layer_spec(down, l)),
    ]
    args = [a for a, _ in operands]
    in_specs = [s if s is not None else _const_spec(a.shape) for a, s in operands]
    scratch = [
        pltpu.VMEM((T, d), _BF16),
        pltpu.VMEM((2, 2, FFN_COL_BLOCK // LANES, QK_HALO + T, LANES), _F32),
        pltpu.VMEM((2 * D_FF // LANES, QK_HALO, LANES), _F32),
        pltpu.VMEM((T, D_FF), _BF16),
    ]
    return pl.pallas_call(
        _ffn_kernel,
        grid=(bsz, seq // T),
        in_specs=in_specs,
        out_specs=pl.BlockSpec((1, T, d), lambda b, t: (b, t, 0)),
        out_shape=jax.ShapeDtypeStruct(x.shape, x.dtype),
        scratch_shapes=scratch,
        compiler_params=pltpu.CompilerParams(
            dimension_semantics=("arbitrary", "arbitrary"),
            vmem_limit_bytes=VMEM_LIMIT_BYTES),
        name="ffn",
    )(*args)


def kernel(x, c, ada_w, ada_b, mix_pre_g, mix_post_g, w_in, qk_conv_w, qk_conv_b, igate_b, fgate_b, ml_norm_g, cv_dw_w, cv_dw_b, cv_ln_g, cv_ln_b, w_out, ffn_pre_g, ffn_post_g, ffn_up, ffn_conv_w, ffn_conv_b, ffn_down):
    bsz, seq, d = x.shape
    depth = ada_w.shape[0]
    assert d == D_MODEL and seq % SEQ_TILE == 0
    mod_all = _modulation(c, ada_w, ada_b).reshape(depth, bsz, 1, N_MOD * d)
    w_main, w_glu = _pack_in_proj(w_in)
    w_out16 = w_out.astype(_BF16)
    up16 = ffn_up.astype(_BF16)
    down16 = ffn_down.astype(_BF16)
    for l in range(depth):
        x = _mixer(l, x, mod_all, mix_pre_g[l], mix_post_g[l], w_main, w_glu, qk_conv_w[l],
                   qk_conv_b[l], igate_b[l], fgate_b[l], ml_norm_g[l], cv_dw_w[l], cv_dw_b[l],
                   cv_ln_g[l], cv_ln_b[l], w_out16)
        x = _ffn(l, x, mod_all, ffn_pre_g[l], ffn_post_g[l], up16, ffn_conv_w[l],
                 ffn_conv_b[l], down16)
    return x
```

```python
import jax
import jax.numpy as jnp
from jax import lax
from jax.experimental import pallas as pl
from jax.experimental.pallas import tpu as pltpu

D_MODEL = 1024
N_HEADS = 4
HEAD_DIM = 256
D_ML = N_HEADS * HEAD_DIM
D_CV = 1024
QK_CONV = 4
CV_KERNEL = 31
ML_CHUNK = 128
D_FF = 2816
FFN_KERNEL = 3
N_MOD = 6
N_GATE = 2 * N_HEADS
EPS = 1e-6

LANES = 128
SUBLANES = 8
SEQ_TILE = 512
QK_HALO = SUBLANES
CV_HALO = 4 * SUBLANES
CV_ROWS = 64
FFN_COL_BLOCK = 256
VMEM_LIMIT_BYTES = 58 * 1024 * 1024

LT_PER_HEAD = HEAD_DIM // LANES
N_CV_LT = D_CV // LANES
GATE_COL = 4 * D_ML
GLU_COL = GATE_COL + LANES
GLU_COL_BLOCK = 512

_F32 = jnp.float32
_BF16 = jnp.bfloat16
_NEG_INF = float("-inf")


def _dot(a, b):
    return jnp.dot(a, b, preferred_element_type=_F32)


def _sigmoid(x):
    return 0.5 * jnp.tanh(0.5 * x) + 0.5


def _silu(x):
    h = 0.5 * x
    return h * jnp.tanh(h) + h


def _rms_scale(x):
    return lax.rsqrt(jnp.mean(x * x, axis=-1, keepdims=True) + EPS)


def _mod_kernel(c_ref, w_ref, b_ref, o_ref):
    cond = _silu(c_ref[...]).astype(_BF16)
    o_ref[0] = _dot(cond, w_ref[0].astype(_BF16)) + b_ref[0]


def _modulation(c, ada_w, ada_b):
    depth, d, n = ada_w.shape
    bsz = c.shape[0]
    nb = n // d
    return pl.pallas_call(
        _mod_kernel,
        grid=(depth, nb),
        in_specs=[
            pl.BlockSpec((bsz, d), lambda l, j: (0, 0)),
            pl.BlockSpec((1, d, d), lambda l, j: (l, 0, j)),
            pl.BlockSpec((1, 1, d), lambda l, j: (l, 0, j)),
        ],
        out_specs=pl.BlockSpec((1, bsz, d), lambda l, j: (l, 0, j)),
        out_shape=jax.ShapeDtypeStruct((depth, bsz, n), _F32),
        compiler_params=pltpu.CompilerParams(
            dimension_semantics=("arbitrary", "arbitrary")),
        name="modulation",
    )(c, ada_w, ada_b.reshape(depth, 1, n))


def _mixer_kernel(x_ref, mod_ref, pre_g_ref, post_g_ref, w_main_ref, gate_b_ref,
                  qkc_w_ref, qkc_b_ref, mln_g_ref, cv_w_ref, cv_b_ref, ln_g_ref, ln_b_ref,
                  w_out_ref, out_ref,
                  u_s, proj_buf, q_s, kt_s, cv_buf, y_s, hy_s, cn_s, m_s):
    T = SEQ_TILE
    D = D_MODEL
    L = ML_CHUNK
    t_idx = pl.program_id(1)

    @pl.when(t_idx == 0)
    def _():
        proj_buf[:, 0:QK_HALO, :] = jnp.zeros((proj_buf.shape[0], QK_HALO, LANES), _F32)
        cv_buf[:, 0:CV_HALO, :] = jnp.zeros((cv_buf.shape[0], CV_HALO, LANES), _F32)
        cn_s[...] = jnp.zeros(cn_s.shape, _F32)
        m_s[...] = jnp.zeros(m_s.shape, _F32)

    x = x_ref[0]
    mod = mod_ref[0, 0]
    sh1 = mod[:, 0:D]
    sc1 = mod[:, D:2 * D]
    g1 = mod[:, 2 * D:3 * D]
    u = (x * _rms_scale(x)) * pre_g_ref[...] * (1.0 + sc1) + sh1
    u_s[...] = u.astype(_BF16)

    for jb in range(D_CV // GLU_COL_BLOCK):
        c0 = GLU_COL + jb * GLU_COL_BLOCK
        a = _dot(u_s[...], w_main_ref[0, :, c0:c0 + GLU_COL_BLOCK])
        gt = _dot(u_s[...], w_main_ref[0, :, c0 + D_CV:c0 + D_CV + GLU_COL_BLOCK])
        glu = a * _sigmoid(gt)
        for jj in range(GLU_COL_BLOCK // LANES):
            cv_buf[jb * (GLU_COL_BLOCK // LANES) + jj, CV_HALO:CV_HALO + T, :] = (
                glu[:, jj * LANES:(jj + 1) * LANES])
    cv_off = CV_HALO - (CV_KERNEL - 1)
    for j in range(N_CV_LT):
        for i in range(T // CV_ROWS):
            r0 = i * CV_ROWS + cv_off
            acc = cv_b_ref[j] + cv_w_ref[j, 0:1, :] * cv_buf[j, r0:r0 + CV_ROWS, :]
            for k in range(1, CV_KERNEL):
                acc = acc + cv_w_ref[j, k:k + 1, :] * cv_buf[j, r0 + k:r0 + k + CV_ROWS, :]
            y_s[j, i * CV_ROWS:(i + 1) * CV_ROWS, :] = acc
    cv_buf[:, 0:CV_HALO, :] = cv_buf[:, T:T + CV_HALO, :]

    tot = y_s[0]
    for j in range(1, N_CV_LT):
        tot = tot + y_s[j]
    mu = jnp.broadcast_to(jnp.sum(tot, axis=-1, keepdims=True) * (1.0 / D_CV), (T, LANES))
    sq = jnp.square(y_s[0] - mu)
    for j in range(1, N_CV_LT):
        sq = sq + jnp.square(y_s[j] - mu)
    var = jnp.sum(sq, axis=-1, keepdims=True) * (1.0 / D_CV)
    inv = jnp.broadcast_to(lax.rsqrt(var + EPS), (T, LANES))
    for j in range(N_CV_LT):
        lcols = slice(j * LANES, (j + 1) * LANES)
        yn = (y_s[j] - mu) * inv * ln_g_ref[:, lcols] + ln_b_ref[:, lcols]
        hy_s[:, D_ML + j * LANES:D_ML + (j + 1) * LANES] = _silu(yn).astype(_BF16)

    gates = _dot(u_s[...], w_main_ref[0, :, GATE_COL:GATE_COL + LANES]) + gate_b_ref[...]
    lane_t = lax.broadcasted_iota(jnp.int32, (T, LANES), 1)
    log_f = jnp.minimum(gates, 0.0) - jnp.log1p(jnp.exp(-jnp.abs(gates)))
    z = jnp.where(lane_t < N_HEADS, gates, log_f)
    row_i = lax.broadcasted_iota(jnp.int32, (L, L), 0)
    col_i = lax.broadcasted_iota(jnp.int32, (L, L), 1)
    causal = row_i >= col_i
    tril = causal.astype(_F32)
    lane_l = lax.broadcasted_iota(jnp.int32, (L, LANES), 1)
    zb_chunks, zbt_chunks = [], []
    for c in range(T // L):
        zc = z[c * L:(c + 1) * L, :]
        csum = jnp.dot(tril, zc, precision=lax.Precision.HIGHEST, preferred_element_type=_F32)
        zb = jnp.where(lane_l < N_HEADS, zc, csum)
        zb_chunks.append(zb)
        zbt_chunks.append(zb.T)

    off = QK_HALO - (QK_CONV - 1)

    def qk_conv(slab, col0):
        cols = slice(col0, col0 + LANES)
        acc = qkc_b_ref[:, cols] + qkc_w_ref[0:1, cols] * proj_buf[slab, off:off + T, :]
        for k in range(1, QK_CONV):
            acc = acc + qkc_w_ref[k:k + 1, cols] * proj_buf[slab, off + k:off + k + T, :]
        return _silu(acc)

    ones_blk = jnp.ones((L, LANES), _BF16)
    mix = _dot(hy_s[:, D_ML:D_ML + D_CV], w_out_ref[0, D_ML:D_ML + D_CV, :])
    for h in range(N_HEADS):
        hc = slice(h * HEAD_DIM, (h + 1) * HEAD_DIM)
        slab_q = h * 4 * LT_PER_HEAD
        slab_k = slab_q + LT_PER_HEAD
        slab_v = slab_k + LT_PER_HEAD
        slab_o = slab_v + LT_PER_HEAD
        for g in range(4):
            c0 = g * D_ML + h * HEAD_DIM
            res = _dot(u_s[...], w_main_ref[0, :, c0:c0 + HEAD_DIM])
            for jj in range(LT_PER_HEAD):
                proj_buf[slab_q + g * LT_PER_HEAD + jj, QK_HALO:QK_HALO + T, :] = (
                    res[:, jj * LANES:(jj + 1) * LANES])
        for jj in range(LT_PER_HEAD):
            lt = h * LT_PER_HEAD + jj
            q_s[:, lt * LANES:(lt + 1) * LANES] = qk_conv(slab_q + jj, lt * LANES).astype(_BF16)
            kt_s[lt * LANES:(lt + 1) * LANES, :] = (
                qk_conv(slab_k + jj, D_ML + lt * LANES) * (HEAD_DIM ** -0.5)).T
        proj_buf[slab_q:slab_v, 0:QK_HALO, :] = proj_buf[slab_q:slab_v, T:T + QK_HALO, :]

        cn = cn_s[h]
        m_prev = m_s[h, 0:1, :]
        for c in range(T // L):
            r = slice(c * L, (c + 1) * L)
            pr = slice(QK_HALO + c * L, QK_HALO + (c + 1) * L)
            zb, zbt = zb_chunks[c], zbt_chunks[c]
            b_rep = jnp.broadcast_to(zb[:, N_HEADS + h:N_HEADS + h + 1], (L, LANES))
            i_row = zbt[h:h + 1, :]
            b_row = zbt[N_HEADS + h:N_HEADS + h + 1, :]
            q = q_s[r, hc]
            kt = kt_s[hc, r]
            v_aug = jnp.concatenate(
                [proj_buf[slab_v + jj, pr, :].astype(_BF16)
                 for jj in range(LT_PER_HEAD)] + [ones_blk], axis=1)
            dmat = jnp.where(causal, b_rep + (i_row - b_row), _NEG_INF)
            bm = b_rep + m_prev
            m_t = jnp.maximum(bm, jnp.broadcast_to(jnp.max(dmat, axis=-1, keepdims=True), (L, LANES)))
            s = _dot(q, kt.astype(_BF16)) * jnp.exp(dmat - m_t)
            intra = _dot(s.astype(_BF16), v_aug)
            inter = _dot(q, cn.astype(_BF16))
            w_inter = jnp.exp(bm - m_t)
            comb = [w_inter * inter[:, jj * LANES:(jj + 1) * LANES] + intra[:, jj * LANES:(jj + 1) * LANES]
                    for jj in range(LT_PER_HEAD + 1)]
            den = comb[LT_PER_HEAD]
            inv_den = 1.0 / jnp.maximum(jnp.abs(den), jnp.exp(-m_t))
            hh = [comb[jj] * inv_den for jj in range(LT_PER_HEAD)]
            hsq = hh[0] * hh[0]
            for jj in range(1, LT_PER_HEAD):
                hsq = hsq + hh[jj] * hh[jj]
            rs = jnp.broadcast_to(
                lax.rsqrt(jnp.sum(hsq, axis=-1, keepdims=True) * (1.0 / HEAD_DIM) + EPS), (L, LANES))
            for jj in range(LT_PER_HEAD):
                cols = slice(h * HEAD_DIM + jj * LANES, h * HEAD_DIM + (jj + 1) * LANES)
                o_pre = proj_buf[slab_o + jj, pr, :]
                hy_s[r, cols] = (hh[jj] * rs * mln_g_ref[:, cols] * _sigmoid(o_pre)).astype(_BF16)
            b_last = b_rep[L - 1:L, :]
            g_row = b_last - b_row + i_row
            m_new = jnp.maximum(
                b_last + m_prev,
                jnp.broadcast_to(jnp.max(g_row, axis=-1, keepdims=True), (1, LANES)))
            kwt = kt * jnp.exp(g_row - m_new)
            upd = _dot(kwt.astype(_BF16), v_aug)
            decay = jnp.exp(b_last + m_prev - m_new)
            cn = jnp.concatenate(
                [decay * cn[:, jj * LANES:(jj + 1) * LANES] + upd[:, jj * LANES:(jj + 1) * LANES]
                 for jj in range(LT_PER_HEAD + 1)], axis=1)
            m_prev = m_new
        cn_s[h] = cn
        m_s[h] = jnp.broadcast_to(m_prev, (SUBLANES, LANES))
        mix = mix + _dot(hy_s[:, hc], w_out_ref[0, hc, :])

    out_ref[0] = x + g1 * (mix * _rms_scale(mix) * post_g_ref[...])


def _const_spec(shape, index=None):
    index = index if index is not None else (0,) * len(shape)
    return pl.BlockSpec(shape, lambda b, t: index, pipeline_mode=pl.Buffered(1))


def _layer_spec(arr, l):
    return _const_spec((1,) + arr.shape[1:], (l,) + (0,) * (arr.ndim - 1))


def _pack_in_proj(w_in):
    depth, d, _ = w_in.shape
    pad = jnp.zeros((depth, d, LANES - N_GATE), w_in.dtype)
    packed = jnp.concatenate(
        [w_in[:, :, 0:GATE_COL + N_GATE], pad, w_in[:, :, GATE_COL + N_GATE:]], axis=-1)
    return packed.astype(_BF16)


def _mixer(l, x, mod_all, pre_g, post_g, w_main, qkc_w, qkc_b, igate_b, fgate_b, mln_g,
           cv_w, cv_b, ln_g, ln_b, w_out):
    bsz, seq, d = x.shape
    T = SEQ_TILE
    gate_b = jnp.pad(jnp.concatenate([igate_b, fgate_b]), (0, LANES - N_GATE)).reshape(1, LANES)
    cv_w3 = jnp.pad(cv_w, ((0, 4 * SUBLANES - CV_KERNEL), (0, 0)))
    cv_w3 = cv_w3.reshape(4 * SUBLANES, N_CV_LT, LANES).transpose(1, 0, 2)
    cv_b3 = cv_b.reshape(N_CV_LT, 1, LANES)
    qkc_w8 = jnp.pad(qkc_w, ((0, SUBLANES - QK_CONV), (0, 0)))

    operands = [
        (x, pl.BlockSpec((1, T, d), lambda b, t: (b, t, 0))),
        (mod_all, pl.BlockSpec((1, 1, 1, N_MOD * d), lambda b, t: (l, b, 0, 0))),
        (pre_g.reshape(1, d), None),
        (post_g.reshape(1, d), None),
        (w_main, _layer_spec(w_main, l)),
        (gate_b, None),
        (qkc_w8, None), (qkc_b.reshape(1, 2 * D_ML), None), (mln_g.reshape(1, D_ML), None),
        (cv_w3, None), (cv_b3, None), (ln_g.reshape(1, D_CV), None), (ln_b.reshape(1, D_CV), None),
        (w_out, _layer_spec(w_out, l)),
    ]
    args = [a for a, _ in operands]
    in_specs = [s if s is not None else _const_spec(a.shape) for a, s in operands]
    scratch = [
        pltpu.VMEM((T, d), _BF16),
        pltpu.VMEM((4 * D_ML // LANES, QK_HALO + T, LANES), _F32),
        pltpu.VMEM((T, D_ML), _BF16),
        pltpu.VMEM((D_ML, T), _F32),
        pltpu.VMEM((N_CV_LT, CV_HALO + T, LANES), _F32),
        pltpu.VMEM((N_CV_LT, T, LANES), _F32),
        pltpu.VMEM((T, D_ML + D_CV), _BF16),
        pltpu.VMEM((N_HEADS, HEAD_DIM, HEAD_DIM + LANES), _F32),
        pltpu.VMEM((N_HEADS, SUBLANES, LANES), _F32),
    ]
    return pl.pallas_call(
        _mixer_kernel,
        grid=(bsz, seq // T),
        in_specs=in_specs,
        out_specs=pl.BlockSpec((1, T, d), lambda b, t: (b, t, 0)),
        out_shape=jax.ShapeDtypeStruct(x.shape, x.dtype),
        scratch_shapes=scratch,
        compiler_params=pltpu.CompilerParams(
            dimension_semantics=("arbitrary", "arbitrary"),
            vmem_limit_bytes=VMEM_LIMIT_BYTES),
        name="mixer",
    )(*args)


def _ffn_kernel(x_ref, mod_ref, pre_g_ref, post_g_ref, up_ref, cw_ref, cb_ref, down_ref,
                out_ref, u_s, buf_s, halo_s, f_s):
    T = SEQ_TILE
    D = D_MODEL
    W = FFN_COL_BLOCK
    t_idx = pl.program_id(1)

    @pl.when(t_idx == 0)
    def _():
        halo_s[...] = jnp.zeros(halo_s.shape, _F32)

    x = x_ref[0]
    mod = mod_ref[0, 0]
    sh2 = mod[:, 3 * D:4 * D]
    sc2 = mod[:, 4 * D:5 * D]
    g2 = mod[:, 5 * D:6 * D]
    u = (x * _rms_scale(x)) * pre_g_ref[...] * (1.0 + sc2) + sh2
    u_s[...] = u.astype(_BF16)

    off = QK_HALO - (FFN_KERNEL - 1)
    n_lt = W // LANES

    def up_conv(slot, part, col0):
        res = _dot(u_s[...], up_ref[0, :, col0:col0 + W])
        out = []
        for jj in range(n_lt):
            lt = col0 // LANES + jj
            cols = slice(lt * LANES, (lt + 1) * LANES)
            buf = buf_s.at[slot, part, jj]
            buf[0:QK_HALO, :] = halo_s[lt]
            buf[QK_HALO:QK_HALO + T, :] = res[:, jj * LANES:(jj + 1) * LANES]
            halo_s[lt] = buf[T:T + QK_HALO, :]
            acc = cb_ref[:, cols] + cw_ref[0:1, cols] * buf[off:off + T, :]
            for k in range(1, FFN_KERNEL):
                acc = acc + cw_ref[k:k + 1, cols] * buf[off + k:off + k + T, :]
            out.append(acc)
        return out

    for j in range(D_FF // W):
        a = up_conv(j % 2, 0, j * W)
        gt = up_conv(j % 2, 1, D_FF + j * W)
        for jj in range(n_lt):
            f_s[:, j * W + jj * LANES:j * W + (jj + 1) * LANES] = (
                _silu(gt[jj]) * a[jj]).astype(_BF16)

    f = _dot(f_s[...], down_ref[0])
    out_ref[0] = x + g2 * (f * _rms_scale(f) * post_g_ref[...])


def _ffn(l, x, mod_all, pre_g, post_g, up, conv_w, conv_b, down):
    bsz, seq, d = x.shape
    T = SEQ_TILE
    cw8 = jnp.pad(conv_w, ((0, SUBLANES - FFN_KERNEL), (0, 0)))
    operands = [
        (x, pl.BlockSpec((1, T, d), lambda b, t: (b, t, 0))),
        (mod_all, pl.BlockSpec((1, 1, 1, N_MOD * d), lambda b, t: (l, b, 0, 0))),
        (pre_g.reshape(1, d), None),
        (post_g.reshape(1, d), None),
        (up, _layer_spec(up, l)),
        (cw8, None),
        (conv_b.reshape(1, 2 * D_FF), None),
        (down, _layer_spec(down, l)),
    ]
    args = [a for a, _ in operands]
    in_specs = [s if s is not None else _const_spec(a.shape) for a, s in operands]
    scratch = [
        pltpu.VMEM((T, d), _BF16),
        pltpu.VMEM((2, 2, FFN_COL_BLOCK // LANES, QK_HALO + T, LANES), _F32),
        pltpu.VMEM((2 * D_FF // LANES, QK_HALO, LANES), _F32),
        pltpu.VMEM((T, D_FF), _BF16),
    ]
    return pl.pallas_call(
        _ffn_kernel,
        grid=(bsz, seq // T),
        in_specs=in_specs,
        out_specs=pl.BlockSpec((1, T, d), lambda b, t: (b, t, 0)),
        out_shape=jax.ShapeDtypeStruct(x.shape, x.dtype),
        scratch_shapes=scratch,
        compiler_params=pltpu.CompilerParams(
            dimension_semantics=("arbitrary", "arbitrary"),
            vmem_limit_bytes=VMEM_LIMIT_BYTES),
        name="ffn",
    )(*args)


def kernel(x, c, ada_w, ada_b, mix_pre_g, mix_post_g, w_in, qk_conv_w, qk_conv_b, igate_b, fgate_b, ml_norm_g, cv_dw_w, cv_dw_b, cv_ln_g, cv_ln_b, w_out, ffn_pre_g, ffn_post_g, ffn_up, ffn_conv_w, ffn_conv_b, ffn_down):
    bsz, seq, d = x.shape
    depth = ada_w.shape[0]
    assert d == D_MODEL and seq % SEQ_TILE == 0
    mod_all = _modulation(c, ada_w, ada_b).reshape(depth, bsz, 1, N_MOD * d)
    w_main = _pack_in_proj(w_in)
    w_out16 = w_out.astype(_BF16)
    up16 = ffn_up.astype(_BF16)
    down16 = ffn_down.astype(_BF16)
    for l in range(depth):
        x = _mixer(l, x, mod_all, mix_pre_g[l], mix_post_g[l], w_main, qk_conv_w[l],
                   qk_conv_b[l], igate_b[l], fgate_b[l], ml_norm_g[l], cv_dw_w[l], cv_dw_b[l],
                   cv_ln_g[l], cv_ln_b[l], w_out16)
        x = _ffn(l, x, mod_all, ffn_pre_g[l], ffn_post_g[l], up16, ffn_conv_w[l],
                 ffn_conv_b[l], down16)
    return x
```

```python
import jax
import jax.numpy as jnp
from jax import lax
from jax.experimental import pallas as pl
from jax.experimental.pallas import tpu as pltpu

D_MODEL = 1024
N_HEADS = 4
HEAD_DIM = 256
D_ML = N_HEADS * HEAD_DIM
D_CV = 1024
QK_CONV = 4
CV_KERNEL = 31
ML_CHUNK = 128
D_FF = 2816
FFN_KERNEL = 3
N_MOD = 6
N_GATE = 2 * N_HEADS
EPS = 1e-6

LANES = 128
SUBLANES = 8
SEQ_TILE = 512
QK_HALO = SUBLANES
CV_HALO = 4 * SUBLANES
CV_ROWS = 64
FFN_COL_BLOCK = 256
VMEM_LIMIT_BYTES = 58 * 1024 * 1024

LT_PER_HEAD = HEAD_DIM // LANES
N_CV_LT = D_CV // LANES
GATE_COL = 4 * D_ML
GLU_COL = GATE_COL + LANES
GLU_COL_BLOCK = 512

_F32 = jnp.float32
_BF16 = jnp.bfloat16
_NEG_INF = float("-inf")


def _dot(a, b):
    return jnp.dot(a, b, preferred_element_type=_F32)


def _sigmoid(x):
    return 0.5 * jnp.tanh(0.5 * x) + 0.5


def _silu(x):
    h = 0.5 * x
    return h * jnp.tanh(h) + h


def _rms_scale(x):
    return lax.rsqrt(jnp.mean(x * x, axis=-1, keepdims=True) + EPS)


def _mod_kernel(c_ref, w_ref, b_ref, o_ref):
    cond = _silu(c_ref[...]).astype(_BF16)
    o_ref[0] = _dot(cond, w_ref[0].astype(_BF16)) + b_ref[0]


def _modulation(c, ada_w, ada_b):
    depth, d, n = ada_w.shape
    bsz = c.shape[0]
    nb = n // d
    return pl.pallas_call(
        _mod_kernel,
        grid=(depth, nb),
        in_specs=[
            pl.BlockSpec((bsz, d), lambda l, j: (0, 0)),
            pl.BlockSpec((1, d, d), lambda l, j: (l, 0, j)),
            pl.BlockSpec((1, 1, d), lambda l, j: (l, 0, j)),
        ],
        out_specs=pl.BlockSpec((1, bsz, d), lambda l, j: (l, 0, j)),
        out_shape=jax.ShapeDtypeStruct((depth, bsz, n), _F32),
        compiler_params=pltpu.CompilerParams(
            dimension_semantics=("arbitrary", "arbitrary")),
        name="modulation",
    )(c, ada_w, ada_b.reshape(depth, 1, n))


def _mixer_kernel(x_ref, mod_ref, pre_g_ref, post_g_ref, w_main_ref, gate_b_ref,
                  qkc_w_ref, qkc_b_ref, mln_g_ref, cv_w_ref, cv_b_ref, ln_g_ref, ln_b_ref,
                  w_out_ref, out_ref,
                  u_s, proj_buf, q_s, kt_s, cv_buf, y_s, hy_s, cn_s, m_s):
    T = SEQ_TILE
    D = D_MODEL
    L = ML_CHUNK
    t_idx = pl.program_id(1)

    @pl.when(t_idx == 0)
    def _():
        proj_buf[:, 0:QK_HALO, :] = jnp.zeros((proj_buf.shape[0], QK_HALO, LANES), _F32)
        cv_buf[:, 0:CV_HALO, :] = jnp.zeros((cv_buf.shape[0], CV_HALO, LANES), _F32)
        cn_s[...] = jnp.zeros(cn_s.shape, _F32)
        m_s[...] = jnp.zeros(m_s.shape, _F32)

    x = x_ref[0]
    mod = mod_ref[0, 0]
    sh1 = mod[:, 0:D]
    sc1 = mod[:, D:2 * D]
    g1 = mod[:, 2 * D:3 * D]
    u = (x * _rms_scale(x)) * pre_g_ref[...] * (1.0 + sc1) + sh1
    u_s[...] = u.astype(_BF16)

    for jb in range(D_CV // GLU_COL_BLOCK):
        c0 = GLU_COL + jb * GLU_COL_BLOCK
        a = _dot(u_s[...], w_main_ref[0, :, c0:c0 + GLU_COL_BLOCK])
        gt = _dot(u_s[...], w_main_ref[0, :, c0 + D_CV:c0 + D_CV + GLU_COL_BLOCK])
        glu = a * _sigmoid(gt)
        for jj in range(GLU_COL_BLOCK // LANES):
            cv_buf[jb * (GLU_COL_BLOCK // LANES) + jj, CV_HALO:CV_HALO + T, :] = (
                glu[:, jj * LANES:(jj + 1) * LANES])
    cv_off = CV_HALO - (CV_KERNEL - 1)
    for j in range(N_CV_LT):
        for i in range(T // CV_ROWS):
            r0 = i * CV_ROWS + cv_off
            acc = cv_b_ref[j] + cv_w_ref[j, 0:1, :] * cv_buf[j, r0:r0 + CV_ROWS, :]
            for k in range(1, CV_KERNEL):
                acc = acc + cv_w_ref[j, k:k + 1, :] * cv_buf[j, r0 + k:r0 + k + CV_ROWS, :]
            y_s[j, i * CV_ROWS:(i + 1) * CV_ROWS, :] = acc
    cv_buf[:, 0:CV_HALO, :] = cv_buf[:, T:T + CV_HALO, :]

    tot = y_s[0]
    for j in range(1, N_CV_LT):
        tot = tot + y_s[j]
    mu = jnp.broadcast_to(jnp.sum(tot, axis=-1, keepdims=True) * (1.0 / D_CV), (T, LANES))
    sq = jnp.square(y_s[0] - mu)
    for j in range(1, N_CV_LT):
        sq = sq + jnp.square(y_s[j] - mu)
    var = jnp.sum(sq, axis=-1, keepdims=True) * (1.0 / D_CV)
    inv = jnp.broadcast_to(lax.rsqrt(var + EPS), (T, LANES))
    for j in range(N_CV_LT):
        lcols = slice(j * LANES, (j + 1) * LANES)
        yn = (y_s[j] - mu) * inv * ln_g_ref[:, lcols] + ln_b_ref[:, lcols]
        hy_s[:, D_ML + j * LANES:D_ML + (j + 1) * LANES] = _silu(yn).astype(_BF16)

    gates = _dot(u_s[...], w_main_ref[0, :, GATE_COL:GATE_COL + LANES]) + gate_b_ref[...]
    lane_t = lax.broadcasted_iota(jnp.int32, (T, LANES), 1)
    log_f = jnp.minimum(gates, 0.0) - jnp.log1p(jnp.exp(-jnp.abs(gates)))
    z = jnp.where(lane_t < N_HEADS, gates, log_f)
    row_i = lax.broadcasted_iota(jnp.int32, (L, L), 0)
    col_i = lax.broadcasted_iota(jnp.int32, (L, L), 1)
    causal = row_i >= col_i
    tril = causal.astype(_F32)
    lane_l = lax.broadcasted_iota(jnp.int32, (L, LANES), 1)
    zb_chunks, zbt_chunks = [], []
    for c in range(T // L):
        zc = z[c * L:(c + 1) * L, :]
        csum = jnp.dot(tril, zc, precision=lax.Precision.HIGHEST, preferred_element_type=_F32)
        zb = jnp.where(lane_l < N_HEADS, zc, csum)
        zb_chunks.append(zb)
        zbt_chunks.append(zb.T)

    off = QK_HALO - (QK_CONV - 1)

    def qk_conv(slab, col0):
        cols = slice(col0, col0 + LANES)
        acc = qkc_b_ref[:, cols] + qkc_w_ref[0:1, cols] * proj_buf[slab, off:off + T, :]
        for k in range(1, QK_CONV):
            acc = acc + qkc_w_ref[k:k + 1, cols] * proj_buf[slab, off + k:off + k + T, :]
        return _silu(acc)

    ones_blk = jnp.ones((L, LANES), _BF16)
    mix = None
    for h in range(N_HEADS):
        hc = slice(h * HEAD_DIM, (h + 1) * HEAD_DIM)
        slab_q = h * 4 * LT_PER_HEAD
        slab_k = slab_q + LT_PER_HEAD
        slab_v = slab_k + LT_PER_HEAD
        slab_o = slab_v + LT_PER_HEAD
        for g in range(4):
            c0 = g * D_ML + h * HEAD_DIM
            res = _dot(u_s[...], w_main_ref[0, :, c0:c0 + HEAD_DIM])
            for jj in range(LT_PER_HEAD):
                proj_buf[slab_q + g * LT_PER_HEAD + jj, QK_HALO:QK_HALO + T, :] = (
                    res[:, jj * LANES:(jj + 1) * LANES])
        for jj in range(LT_PER_HEAD):
            lt = h * LT_PER_HEAD + jj
            q_s[:, lt * LANES:(lt + 1) * LANES] = qk_conv(slab_q + jj, lt * LANES).astype(_BF16)
            kt_s[lt * LANES:(lt + 1) * LANES, :] = (
                qk_conv(slab_k + jj, D_ML + lt * LANES) * (HEAD_DIM ** -0.5)).T
        proj_buf[slab_q:slab_v, 0:QK_HALO, :] = proj_buf[slab_q:slab_v, T:T + QK_HALO, :]

        cn = cn_s[h]
        m_prev = m_s[h, 0:1, :]
        for c in range(T // L):
            r = slice(c * L, (c + 1) * L)
            pr = slice(QK_HALO + c * L, QK_HALO + (c + 1) * L)
            zb, zbt = zb_chunks[c], zbt_chunks[c]
            b_rep = jnp.broadcast_to(zb[:, N_HEADS + h:N_HEADS + h + 1], (L, LANES))
            i_row = zbt[h:h + 1, :]
            b_row = zbt[N_HEADS + h:N_HEADS + h + 1, :]
            q = q_s[r, hc]
            kt = kt_s[hc, r]
            v_aug = jnp.concatenate(
                [proj_buf[slab_v + jj, pr, :].astype(_BF16)
                 for jj in range(LT_PER_HEAD)] + [ones_blk], axis=1)
            dmat = jnp.where(causal, b_rep + (i_row - b_row), _NEG_INF)
            bm = b_rep + m_prev
            m_t = jnp.maximum(bm, jnp.broadcast_to(jnp.max(dmat, axis=-1, keepdims=True), (L, LANES)))
            s = _dot(q, kt.astype(_BF16)) * jnp.exp(dmat - m_t)
            intra = _dot(s.astype(_BF16), v_aug)
            inter = _dot(q, cn.astype(_BF16))
            w_inter = jnp.exp(bm - m_t)
            comb = [w_inter * inter[:, jj * LANES:(jj + 1) * LANES] + intra[:, jj * LANES:(jj + 1) * LANES]
                    for jj in range(LT_PER_HEAD + 1)]
            den = comb[LT_PER_HEAD]
            inv_den = 1.0 / jnp.maximum(jnp.abs(den), jnp.exp(-m_t))
            hh = [comb[jj] * inv_den for jj in range(LT_PER_HEAD)]
            hsq = hh[0] * hh[0]
            for jj in range(1, LT_PER_HEAD):
                hsq = hsq + hh[jj] * hh[jj]
            rs = jnp.broadcast_to(
                lax.rsqrt(jnp.sum(hsq, axis=-1, keepdims=True) * (1.0 / HEAD_DIM) + EPS), (L, LANES))
            for jj in range(LT_PER_HEAD):
                cols = slice(h * HEAD_DIM + jj * LANES, h * HEAD_DIM + (jj + 1) * LANES)
                o_pre = proj_buf[slab_o + jj, pr, :]
                hy_s[r, cols] = (hh[jj] * rs * mln_g_ref[:, cols] * _sigmoid(o_pre)).astype(_BF16)
            b_last = b_rep[L - 1:L, :]
            g_row = b_last - b_row + i_row
            m_new = jnp.maximum(
                b_last + m_prev,
                jnp.broadcast_to(jnp.max(g_row, axis=-1, keepdims=True), (1, LANES)))
            kwt = kt * jnp.exp(g_row - m_new)
            upd = _dot(kwt.astype(_BF16), v_aug)
            decay = jnp.exp(b_last + m_prev - m_new)
            cn = jnp.concatenate(
                [decay * cn[:, jj * LANES:(jj + 1) * LANES] + upd[:, jj * LANES:(jj + 1) * LANES]
                 for jj in range(LT_PER_HEAD + 1)], axis=1)
            m_prev = m_new
        cn_s[h] = cn
        m_s[h] = jnp.broadcast_to(m_prev, (SUBLANES, LANES))
        part = _dot(hy_s[:, hc], w_out_ref[0, hc, :])
        mix = part if mix is None else mix + part
    mix = mix + _dot(hy_s[:, D_ML:D_ML + D_CV], w_out_ref[0, D_ML:D_ML + D_CV, :])

    out_ref[0] = x + g1 * (mix * _rms_scale(mix) * post_g_ref[...])


def _const_spec(shape, index=None):
    index = index if index is not None else (0,) * len(shape)
    return pl.BlockSpec(shape, lambda b, t: index, pipeline_mode=pl.Buffered(1))


def _layer_spec(arr, l):
    return _const_spec((1,) + arr.shape[1:], (l,) + (0,) * (arr.ndim - 1))


def _pack_in_proj(w_in):
    depth, d, n_in = w_in.shape
    w16 = w_in.astype(_BF16)
    packed = jnp.zeros((depth, d, n_in + LANES - N_GATE), _BF16)
    packed = packed.at[:, :, 0:GATE_COL + N_GATE].set(w16[:, :, 0:GATE_COL + N_GATE])
    return packed.at[:, :, GLU_COL:].set(w16[:, :, GATE_COL + N_GATE:])


def _mixer(l, x, mod_all, pre_g, post_g, w_main, qkc_w, qkc_b, igate_b, fgate_b, mln_g,
           cv_w, cv_b, ln_g, ln_b, w_out):
    bsz, seq, d = x.shape
    T = SEQ_TILE
    gate_b = jnp.pad(jnp.concatenate([igate_b, fgate_b]), (0, LANES - N_GATE)).reshape(1, LANES)
    cv_w3 = jnp.pad(cv_w, ((0, 4 * SUBLANES - CV_KERNEL), (0, 0)))
    cv_w3 = cv_w3.reshape(4 * SUBLANES, N_CV_LT, LANES).transpose(1, 0, 2)
    cv_b3 = cv_b.reshape(N_CV_LT, 1, LANES)
    qkc_w8 = jnp.pad(qkc_w, ((0, SUBLANES - QK_CONV), (0, 0)))

    operands = [
        (x, pl.BlockSpec((1, T, d), lambda b, t: (b, t, 0))),
        (mod_all, pl.BlockSpec((1, 1, 1, N_MOD * d), lambda b, t: (l, b, 0, 0))),
        (pre_g.reshape(1, d), None),
        (post_g.reshape(1, d), None),
        (w_main, _layer_spec(w_main, l)),
        (gate_b, None),
        (qkc_w8, None), (qkc_b.reshape(1, 2 * D_ML), None), (mln_g.reshape(1, D_ML), None),
        (cv_w3, None), (cv_b3, None), (ln_g.reshape(1, D_CV), None), (ln_b.reshape(1, D_CV), None),
        (w_out, _layer_spec(w_out, l)),
    ]
    args = [a for a, _ in operands]
    in_specs = [s if s is not None else _const_spec(a.shape) for a, s in operands]
    scratch = [
        pltpu.VMEM((T, d), _BF16),
        pltpu.VMEM((4 * D_ML // LANES, QK_HALO + T, LANES), _F32),
        pltpu.VMEM((T, D_ML), _BF16),
        pltpu.VMEM((D_ML, T), _F32),
        pltpu.VMEM((N_CV_LT, CV_HALO + T, LANES), _F32),
        pltpu.VMEM((N_CV_LT, T, LANES), _F32),
        pltpu.VMEM((T, D_ML + D_CV), _BF16),
        pltpu.VMEM((N_HEADS, HEAD_DIM, HEAD_DIM + LANES), _F32),
        pltpu.VMEM((N_HEADS, SUBLANES, LANES), _F32),
    ]
    return pl.pallas_call(
        _mixer_kernel,
        grid=(bsz, seq // T),
        in_specs=in_specs,
        out_specs=pl.BlockSpec((1, T, d), lambda b, t: (b, t, 0)),
        out_shape=jax.ShapeDtypeStruct(x.shape, x.dtype),
        scratch_shapes=scratch,
        compiler_params=pltpu.CompilerParams(
            dimension_semantics=("arbitrary", "arbitrary"),
            vmem_limit_bytes=VMEM_LIMIT_BYTES),
        name="mixer",
    )(*args)


def _ffn_kernel(x_ref, mod_ref, pre_g_ref, post_g_ref, up_ref, cw_ref, cb_ref, down_ref,
                out_ref, u_s, buf_s, halo_s, f_s):
    T = SEQ_TILE
    D = D_MODEL
    W = FFN_COL_BLOCK
    t_idx = pl.program_id(1)

    @pl.when(t_idx == 0)
    def _():
        halo_s[...] = jnp.zeros(halo_s.shape, _F32)

    x = x_ref[0]
    mod = mod_ref[0, 0]
    sh2 = mod[:, 3 * D:4 * D]
    sc2 = mod[:, 4 * D:5 * D]
    g2 = mod[:, 5 * D:6 * D]
    u = (x * _rms_scale(x)) * pre_g_ref[...] * (1.0 + sc2) + sh2
    u_s[...] = u.astype(_BF16)

    off = QK_HALO - (FFN_KERNEL - 1)
    n_lt = W // LANES

    def up_conv(slot, part, col0):
        res = _dot(u_s[...], up_ref[0, :, col0:col0 + W])
        out = []
        for jj in range(n_lt):
            lt = col0 // LANES + jj
            cols = slice(lt * LANES, (lt + 1) * LANES)
            buf = buf_s.at[slot, part, jj]
            buf[0:QK_HALO, :] = halo_s[lt]
            buf[QK_HALO:QK_HALO + T, :] = res[:, jj * LANES:(jj + 1) * LANES]
            halo_s[lt] = buf[T:T + QK_HALO, :]
            acc = cb_ref[:, cols] + cw_ref[0:1, cols] * buf[off:off + T, :]
            for k in range(1, FFN_KERNEL):
                acc = acc + cw_ref[k:k + 1, cols] * buf[off + k:off + k + T, :]
            out.append(acc)
        return out

    for j in range(D_FF // W):
        a = up_conv(j % 2, 0, j * W)
        gt = up_conv(j % 2, 1, D_FF + j * W)
        for jj in range(n_lt):
            f_s[:, j * W + jj * LANES:j * W + (jj + 1) * LANES] = (
                _silu(gt[jj]) * a[jj]).astype(_BF16)

    f = _dot(f_s[...], down_ref[0])
    out_ref[0] = x + g2 * (f * _rms_scale(f) * post_g_ref[...])


def _ffn(l, x, mod_all, pre_g, post_g, up, conv_w, conv_b, down):
    bsz, seq, d = x.shape
    T = SEQ_TILE
    cw8 = jnp.pad(conv_w, ((0, SUBLANES - FFN_KERNEL), (0, 0)))
    operands = [
        (x, pl.BlockSpec((1, T, d), lambda b, t: (b, t, 0))),
        (mod_all, pl.BlockSpec((1, 1, 1, N_MOD * d), lambda b, t: (l, b, 0, 0))),
        (pre_g.reshape(1, d), None),
        (post_g.reshape(1, d), None),
        (up, _layer_spec(up, l)),
        (cw8, None),
        (conv_b.reshape(1, 2 * D_FF), None),
        (down, _layer_spec(down, l)),
    ]
    args = [a for a, _ in operands]
    in_specs = [s if s is not None else _const_spec(a.shape) for a, s in operands]
    scratch = [
        pltpu.VMEM((T, d), _BF16),
        pltpu.VMEM((2, 2, FFN_COL_BLOCK // LANES, QK_HALO + T, LANES), _F32),
        pltpu.VMEM((2 * D_FF // LANES, QK_HALO, LANES), _F32),
        pltpu.VMEM((T, D_FF), _BF16),
    ]
    return pl.pallas_call(
        _ffn_kernel,
        grid=(bsz, seq // T),
        in_specs=in_specs,
        out_specs=pl.BlockSpec((1, T, d), lambda b, t: (b, t, 0)),
        out_shape=jax.ShapeDtypeStruct(x.shape, x.dtype),
        scratch_shapes=scratch,
        compiler_params=pltpu.CompilerParams(
            dimension_semantics=("arbitrary", "arbitrary"),
            vmem_limit_bytes=VMEM_LIMIT_BYTES),
        name="ffn",
    )(*args)


def kernel(x, c, ada_w, ada_b, mix_pre_g, mix_post_g, w_in, qk_conv_w, qk_conv_b, igate_b, fgate_b, ml_norm_g, cv_dw_w, cv_dw_b, cv_ln_g, cv_ln_b, w_out, ffn_pre_g, ffn_post_g, ffn_up, ffn_conv_w, ffn_conv_b, ffn_down):
    bsz, seq, d = x.shape
    depth = ada_w.shape[0]
    assert d == D_MODEL and seq % SEQ_TILE == 0
    mod_all = _modulation(c, ada_w, ada_b).reshape(depth, bsz, 1, N_MOD * d)
    w_main = _pack_in_proj(w_in)
    w_out16 = w_out.astype(_BF16)
    up16 = ffn_up.astype(_BF16)
    down16 = ffn_down.astype(_BF16)
    for l in range(depth):
        x = _mixer(l, x, mod_all, mix_pre_g[l], mix_post_g[l], w_main, qk_conv_w[l],
                   qk_conv_b[l], igate_b[l], fgate_b[l], ml_norm_g[l], cv_dw_w[l], cv_dw_b[l],
                   cv_ln_g[l], cv_ln_b[l], w_out16)
        x = _ffn(l, x, mod_all, ffn_pre_g[l], ffn_post_g[l], up16, ffn_conv_w[l],
                 ffn_conv_b[l], down16)
    return x
```

```python
import jax
import jax.numpy as jnp
from jax import lax
from jax.experimental import pallas as pl
from jax.experimental.pallas import tpu as pltpu

D_MODEL = 1024
N_HEADS = 4
HEAD_DIM = 256
D_ML = N_HEADS * HEAD_DIM
D_CV = 1024
QK_CONV = 4
CV_KERNEL = 31
ML_CHUNK = 128
D_FF = 2816
FFN_KERNEL = 3
N_MOD = 6
N_GATE = 2 * N_HEADS
EPS = 1e-6

LANES = 128
SUBLANES = 8
SEQ_TILE = 512
QK_HALO = SUBLANES
CV_HALO = 4 * SUBLANES
CV_ROWS = 64
FFN_COL_BLOCK = 256
FFN_SUB_TILES = 1
VMEM_LIMIT_BYTES = 58 * 1024 * 1024

LT_PER_HEAD = HEAD_DIM // LANES
N_CV_LT = D_CV // LANES
GATE_COL = 4 * D_ML
GLU_COL_BLOCK = 512

_F32 = jnp.float32
_BF16 = jnp.bfloat16
_NEG_INF = float("-inf")


def _dot(a, b):
    return jnp.dot(a, b, preferred_element_type=_F32)


def _sigmoid(x):
    return 0.5 * jnp.tanh(0.5 * x) + 0.5


def _silu(x):
    h = 0.5 * x
    return h * jnp.tanh(h) + h


def _rms_scale(x):
    return lax.rsqrt(jnp.mean(x * x, axis=-1, keepdims=True) + EPS)


def _mod_kernel(c_ref, w_ref, b_ref, o_ref):
    cond = _silu(c_ref[...]).astype(_BF16)
    o_ref[0] = _dot(cond, w_ref[0].astype(_BF16)) + b_ref[0]


def _modulation(c, ada_w, ada_b):
    depth, d, n = ada_w.shape
    bsz = c.shape[0]
    nb = n // d
    return pl.pallas_call(
        _mod_kernel,
        grid=(depth, nb),
        in_specs=[
            pl.BlockSpec((bsz, d), lambda l, j: (0, 0)),
            pl.BlockSpec((1, d, d), lambda l, j: (l, 0, j)),
            pl.BlockSpec((1, 1, d), lambda l, j: (l, 0, j)),
        ],
        out_specs=pl.BlockSpec((1, bsz, d), lambda l, j: (l, 0, j)),
        out_shape=jax.ShapeDtypeStruct((depth, bsz, n), _F32),
        compiler_params=pltpu.CompilerParams(
            dimension_semantics=("arbitrary", "arbitrary")),
        name="modulation",
    )(c, ada_w, ada_b.reshape(depth, 1, n))


def _mixer_kernel(x_ref, mod_ref, pre_g_ref, post_g_ref, w_main_ref, w_glu_ref, gate_b_ref,
                  qkc_w_ref, qkc_b_ref, mln_g_ref, cv_w_ref, cv_b_ref, ln_g_ref, ln_b_ref,
                  w_out_ref, out_ref,
                  u_s, proj_buf, q_s, kt_s, cv_buf, y_s, hy_s, cn_s, m_s):
    T = SEQ_TILE
    D = D_MODEL
    L = ML_CHUNK
    t_idx = pl.program_id(1)

    @pl.when(t_idx == 0)
    def _():
        proj_buf[:, 0:QK_HALO, :] = jnp.zeros((proj_buf.shape[0], QK_HALO, LANES), _F32)
        cv_buf[:, 0:CV_HALO, :] = jnp.zeros((cv_buf.shape[0], CV_HALO, LANES), _F32)
        cn_s[...] = jnp.zeros(cn_s.shape, _F32)
        m_s[...] = jnp.zeros(m_s.shape, _F32)

    x = x_ref[0]
    mod = mod_ref[0, 0]
    sh1 = mod[:, 0:D]
    sc1 = mod[:, D:2 * D]
    g1 = mod[:, 2 * D:3 * D]
    u = (x * _rms_scale(x)) * (pre_g_ref[...] * (1.0 + sc1)) + sh1
    u_s[...] = u.astype(_BF16)

    for jb in range(D_CV // GLU_COL_BLOCK):
        c0 = jb * GLU_COL_BLOCK
        a = _dot(u_s[...], w_glu_ref[0, :, c0:c0 + GLU_COL_BLOCK])
        gt = _dot(u_s[...], w_glu_ref[0, :, c0 + D_CV:c0 + D_CV + GLU_COL_BLOCK])
        glu = a * _sigmoid(gt)
        for jj in range(GLU_COL_BLOCK // LANES):
            cv_buf[jb * (GLU_COL_BLOCK // LANES) + jj, CV_HALO:CV_HALO + T, :] = (
                glu[:, jj * LANES:(jj + 1) * LANES])
    cv_off = CV_HALO - (CV_KERNEL - 1)
    for j in range(N_CV_LT):
        for i in range(T // CV_ROWS):
            r0 = i * CV_ROWS + cv_off
            acc = cv_b_ref[j] + cv_w_ref[j, 0:1, :] * cv_buf[j, r0:r0 + CV_ROWS, :]
            for k in range(1, CV_KERNEL):
                acc = acc + cv_w_ref[j, k:k + 1, :] * cv_buf[j, r0 + k:r0 + k + CV_ROWS, :]
            y_s[j, i * CV_ROWS:(i + 1) * CV_ROWS, :] = acc
    cv_buf[:, 0:CV_HALO, :] = cv_buf[:, T:T + CV_HALO, :]

    tot = y_s[0]
    for j in range(1, N_CV_LT):
        tot = tot + y_s[j]
    mu = jnp.broadcast_to(jnp.sum(tot, axis=-1, keepdims=True) * (1.0 / D_CV), (T, LANES))
    sq = jnp.square(y_s[0] - mu)
    for j in range(1, N_CV_LT):
        sq = sq + jnp.square(y_s[j] - mu)
    var = jnp.sum(sq, axis=-1, keepdims=True) * (1.0 / D_CV)
    inv = jnp.broadcast_to(lax.rsqrt(var + EPS), (T, LANES))
    for j in range(N_CV_LT):
        lcols = slice(j * LANES, (j + 1) * LANES)
        yn = (y_s[j] - mu) * inv * ln_g_ref[:, lcols] + ln_b_ref[:, lcols]
        hy_s[:, D_ML + j * LANES:D_ML + (j + 1) * LANES] = _silu(yn).astype(_BF16)

    gates = _dot(u_s[...], w_main_ref[0, :, GATE_COL:GATE_COL + LANES]) + gate_b_ref[...]
    lane_t = lax.broadcasted_iota(jnp.int32, (T, LANES), 1)
    log_f = jnp.minimum(gates, 0.0) - jnp.log1p(jnp.exp(-jnp.abs(gates)))
    z = jnp.where(lane_t < N_HEADS, gates, log_f)
    row_i = lax.broadcasted_iota(jnp.int32, (L, L), 0)
    col_i = lax.broadcasted_iota(jnp.int32, (L, L), 1)
    causal = row_i >= col_i
    tril = causal.astype(_F32)
    lane_l = lax.broadcasted_iota(jnp.int32, (L, LANES), 1)
    zb_chunks, zbt_chunks = [], []
    for c in range(T // L):
        zc = z[c * L:(c + 1) * L, :]
        csum = jnp.dot(tril, zc, precision=lax.Precision.HIGHEST, preferred_element_type=_F32)
        zb = jnp.where(lane_l < N_HEADS, zc, csum)
        zb_chunks.append(zb)
        zbt_chunks.append(zb.T)

    off = QK_HALO - (QK_CONV - 1)

    def qk_conv(slab, col0):
        cols = slice(col0, col0 + LANES)
        acc = qkc_b_ref[:, cols] + qkc_w_ref[0:1, cols] * proj_buf[slab, off:off + T, :]
        for k in range(1, QK_CONV):
            acc = acc + qkc_w_ref[k:k + 1, cols] * proj_buf[slab, off + k:off + k + T, :]
        return _silu(acc)

    ones_blk = jnp.ones((L, LANES), _BF16)
    mix = None
    for h in range(N_HEADS):
        hc = slice(h * HEAD_DIM, (h + 1) * HEAD_DIM)
        slab_q = h * 4 * LT_PER_HEAD
        slab_k = slab_q + LT_PER_HEAD
        slab_v = slab_k + LT_PER_HEAD
        slab_o = slab_v + LT_PER_HEAD
        for g in range(4):
            c0 = g * D_ML + h * HEAD_DIM
            res = _dot(u_s[...], w_main_ref[0, :, c0:c0 + HEAD_DIM])
            for jj in range(LT_PER_HEAD):
                proj_buf[slab_q + g * LT_PER_HEAD + jj, QK_HALO:QK_HALO + T, :] = (
                    res[:, jj * LANES:(jj + 1) * LANES])
        for jj in range(LT_PER_HEAD):
            lt = h * LT_PER_HEAD + jj
            q_s[:, lt * LANES:(lt + 1) * LANES] = qk_conv(slab_q + jj, lt * LANES).astype(_BF16)
            kt_s[lt * LANES:(lt + 1) * LANES, :] = (
                qk_conv(slab_k + jj, D_ML + lt * LANES) * (HEAD_DIM ** -0.5)).T
        proj_buf[slab_q:slab_v, 0:QK_HALO, :] = proj_buf[slab_q:slab_v, T:T + QK_HALO, :]

        cn = cn_s[h]
        m_prev = m_s[h, 0:1, :]
        for c in range(T // L):
            r = slice(c * L, (c + 1) * L)
            pr = slice(QK_HALO + c * L, QK_HALO + (c + 1) * L)
            zb, zbt = zb_chunks[c], zbt_chunks[c]
            b_rep = jnp.broadcast_to(zb[:, N_HEADS + h:N_HEADS + h + 1], (L, LANES))
            i_row = zbt[h:h + 1, :]
            b_row = zbt[N_HEADS + h:N_HEADS + h + 1, :]
            q = q_s[r, hc]
            kt = kt_s[hc, r]
            v_aug = jnp.concatenate(
                [proj_buf[slab_v + jj, pr, :].astype(_BF16)
                 for jj in range(LT_PER_HEAD)] + [ones_blk], axis=1)
            dmat = jnp.where(causal, b_rep + (i_row - b_row), _NEG_INF)
            bm = b_rep + m_prev
            m_t = jnp.maximum(bm, jnp.broadcast_to(jnp.max(dmat, axis=-1, keepdims=True), (L, LANES)))
            s = _dot(q, kt.astype(_BF16)) * jnp.exp(dmat - m_t)
            intra = _dot(s.astype(_BF16), v_aug)
            inter = _dot(q, cn.astype(_BF16))
            w_inter = jnp.exp(bm - m_t)
            comb = [w_inter * inter[:, jj * LANES:(jj + 1) * LANES] + intra[:, jj * LANES:(jj + 1) * LANES]
                    for jj in range(LT_PER_HEAD + 1)]
            den = comb[LT_PER_HEAD]
            inv_den = 1.0 / jnp.maximum(jnp.abs(den), jnp.exp(-m_t))
            hh = [comb[jj] * inv_den for jj in range(LT_PER_HEAD)]
            hsq = hh[0] * hh[0]
            for jj in range(1, LT_PER_HEAD):
                hsq = hsq + hh[jj] * hh[jj]
            rs = jnp.broadcast_to(
                lax.rsqrt(jnp.sum(hsq, axis=-1, keepdims=True) * (1.0 / HEAD_DIM) + EPS), (L, LANES))
            for jj in range(LT_PER_HEAD):
                cols = slice(h * HEAD_DIM + jj * LANES, h * HEAD_DIM + (jj + 1) * LANES)
                o_pre = proj_buf[slab_o + jj, pr, :]
                hy_s[r, cols] = (hh[jj] * rs * mln_g_ref[:, cols] * _sigmoid(o_pre)).astype(_BF16)
            b_last = b_rep[L - 1:L, :]
            g_row = b_last - b_row + i_row
            m_new = jnp.maximum(
                b_last + m_prev,
                jnp.broadcast_to(jnp.max(g_row, axis=-1, keepdims=True), (1, LANES)))
            kwt = kt * jnp.exp(g_row - m_new)
            upd = _dot(kwt.astype(_BF16), v_aug)
            decay = jnp.exp(b_last + m_prev - m_new)
            cn = jnp.concatenate(
                [decay * cn[:, jj * LANES:(jj + 1) * LANES] + upd[:, jj * LANES:(jj + 1) * LANES]
                 for jj in range(LT_PER_HEAD + 1)], axis=1)
            m_prev = m_new
        cn_s[h] = cn
        m_s[h] = jnp.broadcast_to(m_prev, (SUBLANES, LANES))
        part = _dot(hy_s[:, hc], w_out_ref[0, hc, :])
        mix = part if mix is None else mix + part
    mix = mix + _dot(hy_s[:, D_ML:D_ML + D_CV], w_out_ref[0, D_ML:D_ML + D_CV, :])

    out_ref[0] = x + (mix * _rms_scale(mix)) * (g1 * post_g_ref[...])


def _const_spec(shape, index=None):
    index = index if index is not None else (0,) * len(shape)
    return pl.BlockSpec(shape, lambda b, t: index, pipeline_mode=pl.Buffered(1))


def _layer_spec(arr, l):
    return _const_spec((1,) + arr.shape[1:], (l,) + (0,) * (arr.ndim - 1))


def _mixer(l, x, mod_all, pre_g, post_g, w_main, w_glu, qkc_w, qkc_b, igate_b, fgate_b, mln_g,
           cv_w, cv_b, ln_g, ln_b, w_out):
    bsz, seq, d = x.shape
    T = SEQ_TILE
    gate_b = jnp.pad(jnp.concatenate([igate_b, fgate_b]), (0, LANES - N_GATE)).reshape(1, LANES)
    cv_w3 = jnp.pad(cv_w, ((0, 4 * SUBLANES - CV_KERNEL), (0, 0)))
    cv_w3 = cv_w3.reshape(4 * SUBLANES, N_CV_LT, LANES).transpose(1, 0, 2)
    cv_b3 = cv_b.reshape(N_CV_LT, 1, LANES)
    qkc_w8 = jnp.pad(qkc_w, ((0, SUBLANES - QK_CONV), (0, 0)))

    operands = [
        (x, pl.BlockSpec((1, T, d), lambda b, t: (b, t, 0))),
        (mod_all, pl.BlockSpec((1, 1, 1, N_MOD * d), lambda b, t: (l, b, 0, 0))),
        (pre_g.reshape(1, d), None),
        (post_g.reshape(1, d), None),
        (w_main, _layer_spec(w_main, l)),
        (w_glu, _layer_spec(w_glu, l)),
        (gate_b, None),
        (qkc_w8, None), (qkc_b.reshape(1, 2 * D_ML), None), (mln_g.reshape(1, D_ML), None),
        (cv_w3, None), (cv_b3, None), (ln_g.reshape(1, D_CV), None), (ln_b.reshape(1, D_CV), None),
        (w_out, _layer_spec(w_out, l)),
    ]
    args = [a for a, _ in operands]
    in_specs = [s if s is not None else _const_spec(a.shape) for a, s in operands]
    scratch = [
        pltpu.VMEM((T, d), _BF16),
        pltpu.VMEM((4 * D_ML // LANES, QK_HALO + T, LANES), _F32),
        pltpu.VMEM((T, D_ML), _BF16),
        pltpu.VMEM((D_ML, T), _F32),
        pltpu.VMEM((N_CV_LT, CV_HALO + T, LANES), _F32),
        pltpu.VMEM((N_CV_LT, T, LANES), _F32),
        pltpu.VMEM((T, D_ML + D_CV), _BF16),
        pltpu.VMEM((N_HEADS, HEAD_DIM, HEAD_DIM + LANES), _F32),
        pltpu.VMEM((N_HEADS, SUBLANES, LANES), _F32),
    ]
    return pl.pallas_call(
        _mixer_kernel,
        grid=(bsz, seq // T),
        in_specs=in_specs,
        out_specs=pl.BlockSpec((1, T, d), lambda b, t: (b, t, 0)),
        out_shape=jax.ShapeDtypeStruct(x.shape, x.dtype),
        scratch_shapes=scratch,
        compiler_params=pltpu.CompilerParams(
            dimension_semantics=("arbitrary", "arbitrary"),
            vmem_limit_bytes=VMEM_LIMIT_BYTES),
        name="mixer",
    )(*args)


def _ffn_kernel(x_ref, mod_ref, pre_g_ref, post_g_ref, up_ref, cw_ref, cb_ref, down_ref,
                out_ref, u_s, buf_s, halo_s, f_s):
    T = SEQ_TILE
    D = D_MODEL
    W = FFN_COL_BLOCK
    t_idx = pl.program_id(1)

    @pl.when(t_idx == 0)
    def _():
        halo_s[...] = jnp.zeros(halo_s.shape, _F32)

    mod = mod_ref[0, 0]
    sh2 = mod[:, 3 * D:4 * D]
    sc2 = mod[:, 4 * D:5 * D]
    g2 = mod[:, 5 * D:6 * D]
    pre_gain = pre_g_ref[...] * (1.0 + sc2)
    post_gain = g2 * post_g_ref[...]
    off = QK_HALO - (FFN_KERNEL - 1)
    n_lt = W // LANES

    for sub in range(FFN_SUB_TILES):
        rows = slice(sub * T, (sub + 1) * T)
        x = x_ref[0, rows, :]
        u_s[sub] = ((x * _rms_scale(x)) * pre_gain + sh2).astype(_BF16)

        def up_conv(slot, part, col0):
            res = _dot(u_s[sub], up_ref[0, :, col0:col0 + W])
            out = []
            for jj in range(n_lt):
                lt = col0 // LANES + jj
                cols = slice(lt * LANES, (lt + 1) * LANES)
                buf = buf_s.at[slot, part, jj]
                buf[0:QK_HALO, :] = halo_s[lt]
                buf[QK_HALO:QK_HALO + T, :] = res[:, jj * LANES:(jj + 1) * LANES]
                halo_s[lt] = buf[T:T + QK_HALO, :]
                acc = cb_ref[:, cols] + cw_ref[0:1, cols] * buf[off:off + T, :]
                for k in range(1, FFN_KERNEL):
                    acc = acc + cw_ref[k:k + 1, cols] * buf[off + k:off + k + T, :]
                out.append(acc)
            return out

        for j in range(D_FF // W):
            a = up_conv(j % 2, 0, j * W)
            gt = up_conv(j % 2, 1, D_FF + j * W)
            for jj in range(n_lt):
                f_s[sub, :, j * W + jj * LANES:j * W + (jj + 1) * LANES] = (
                    _silu(gt[jj]) * a[jj]).astype(_BF16)

        f = _dot(f_s[sub], down_ref[0])
        out_ref[0, rows, :] = x + (f * _rms_scale(f)) * post_gain


def _ffn(l, x, mod_all, pre_g, post_g, up, conv_w, conv_b, down):
    bsz, seq, d = x.shape
    T = SEQ_TILE
    step_rows = FFN_SUB_TILES * T
    assert seq % step_rows == 0
    cw8 = jnp.pad(conv_w, ((0, SUBLANES - FFN_KERNEL), (0, 0)))
    operands = [
        (x, pl.BlockSpec((1, step_rows, d), lambda b, t: (b, t, 0))),
        (mod_all, pl.BlockSpec((1, 1, 1, N_MOD * d), lambda b, t: (l, b, 0, 0))),
        (pre_g.reshape(1, d), None),
        (post_g.reshape(1, d), None),
        (up, _layer_spec(up, l)),
        (cw8, None),
        (conv_b.reshape(1, 2 * D_FF), None),
        (down, _layer_spec(down, l)),
    ]
    args = [a for a, _ in operands]
    in_specs = [s if s is not None else _const_spec(a.shape) for a, s in operands]
    scratch = [
        pltpu.VMEM((FFN_SUB_TILES, T, d), _BF16),
        pltpu.VMEM((2, 2, FFN_COL_BLOCK // LANES, QK_HALO + T, LANES), _F32),
        pltpu.VMEM((2 * D_FF // LANES, QK_HALO, LANES), _F32),
        pltpu.VMEM((FFN_SUB_TILES, T, D_FF), _BF16),
    ]
    return pl.pallas_call(
        _ffn_kernel,
        grid=(bsz, seq // step_rows),
        in_specs=in_specs,
        out_specs=pl.BlockSpec((1, step_rows, d), lambda b, t: (b, t, 0)),
        out_shape=jax.ShapeDtypeStruct(x.shape, x.dtype),
        scratch_shapes=scratch,
        compiler_params=pltpu.CompilerParams(
            dimension_semantics=("arbitrary", "arbitrary"),
            vmem_limit_bytes=VMEM_LIMIT_BYTES),
        name="ffn",
    )(*args)


def kernel(x, c, ada_w, ada_b, mix_pre_g, mix_post_g, w_in, qk_conv_w, qk_conv_b, igate_b, fgate_b, ml_norm_g, cv_dw_w, cv_dw_b, cv_ln_g, cv_ln_b, w_out, ffn_pre_g, ffn_post_g, ffn_up, ffn_conv_w, ffn_conv_b, ffn_down):
    bsz, seq, d = x.shape
    depth = ada_w.shape[0]
    assert d == D_MODEL and seq % SEQ_TILE == 0
    mod_all = _modulation(c, ada_w, ada_b).reshape(depth, bsz, 1, N_MOD * d)
    w_main = w_in.astype(_BF16)
    w_glu = w_main[:, :, GATE_COL + N_GATE:]
    w_out16 = w_out.astype(_BF16)
    up16 = ffn_up.astype(_BF16)
    down16 = ffn_down.astype(_BF16)
    for l in range(depth):
        x = _mixer(l, x, mod_all, mix_pre_g[l], mix_post_g[l], w_main, w_glu, qk_conv_w[l],
                   qk_conv_b[l], igate_b[l], fgate_b[l], ml_norm_g[l], cv_dw_w[l], cv_dw_b[l],
                   cv_ln_g[l], cv_ln_b[l], w_out16)
        x = _ffn(l, x, mod_all, ffn_pre_g[l], ffn_post_g[l], up16, ffn_conv_w[l],
                 ffn_conv_b[l], down16)
    return x
```

```python
import jax
import jax.numpy as jnp
from jax import lax
from jax.experimental import pallas as pl
from jax.experimental.pallas import tpu as pltpu

D_MODEL = 1024
N_HEADS = 4
HEAD_DIM = 256
D_ML = N_HEADS * HEAD_DIM
D_CV = 1024
QK_CONV = 4
CV_KERNEL = 31
ML_CHUNK = 128
D_FF = 2816
FFN_KERNEL = 3
N_MOD = 6
N_GATE = 2 * N_HEADS
EPS = 1e-6

LANES = 128
SUBLANES = 8
SEQ_TILE = 512
QK_HALO = SUBLANES
CV_HALO = 4 * SUBLANES
CV_ROWS = 64
FFN_COL_BLOCK = 256
VMEM_LIMIT_BYTES = 58 * 1024 * 1024

LT_PER_HEAD = HEAD_DIM // LANES
N_CV_LT = D_CV // LANES
GATE_COL = 4 * D_ML
GLU_COL_BLOCK = 512

_F32 = jnp.float32
_BF16 = jnp.bfloat16
_NEG_INF = float("-inf")


def _dot(a, b):
    return jnp.dot(a, b, preferred_element_type=_F32)


def _sigmoid(x):
    return 0.5 * jnp.tanh(0.5 * x) + 0.5


def _silu(x):
    h = 0.5 * x
    return h * jnp.tanh(h) + h


def _rms_scale(x):
    return lax.rsqrt(jnp.mean(x * x, axis=-1, keepdims=True) + EPS)


def _mod_kernel(c_ref, w_ref, b_ref, o_ref):
    cond = _silu(c_ref[...]).astype(_BF16)
    o_ref[0] = _dot(cond, w_ref[0].astype(_BF16)) + b_ref[0]


def _modulation(c, ada_w, ada_b):
    depth, d, n = ada_w.shape
    bsz = c.shape[0]
    nb = n // d
    return pl.pallas_call(
        _mod_kernel,
        grid=(depth, nb),
        in_specs=[
            pl.BlockSpec((bsz, d), lambda l, j: (0, 0)),
            pl.BlockSpec((1, d, d), lambda l, j: (l, 0, j)),
            pl.BlockSpec((1, 1, d), lambda l, j: (l, 0, j)),
        ],
        out_specs=pl.BlockSpec((1, bsz, d), lambda l, j: (l, 0, j)),
        out_shape=jax.ShapeDtypeStruct((depth, bsz, n), _F32),
        compiler_params=pltpu.CompilerParams(
            dimension_semantics=("arbitrary", "arbitrary")),
        name="modulation",
    )(c, ada_w, ada_b.reshape(depth, 1, n))


def _mixer_kernel(x_ref, mod_ref, pre_g_ref, post_g_ref, w_main_ref, w_glu_ref, gate_b_ref,
                  qkc_w_ref, qkc_b_ref, mln_g_ref, cv_w_ref, cv_b_ref, ln_g_ref, ln_b_ref,
                  w_out_ref, out_ref,
                  u_s, proj_buf, q_s, kt_s, cv_buf, y_s, hy_s, cn_s, m_s):
    T = SEQ_TILE
    D = D_MODEL
    L = ML_CHUNK
    t_idx = pl.program_id(1)

    @pl.when(t_idx == 0)
    def _():
        proj_buf[:, 0:QK_HALO, :] = jnp.zeros((proj_buf.shape[0], QK_HALO, LANES), _F32)
        cv_buf[:, 0:CV_HALO, :] = jnp.zeros((cv_buf.shape[0], CV_HALO, LANES), _F32)
        cn_s[...] = jnp.zeros(cn_s.shape, _F32)
        m_s[...] = jnp.zeros(m_s.shape, _F32)

    x = x_ref[0]
    mod = mod_ref[0, 0]
    sh1 = mod[:, 0:D]
    sc1 = mod[:, D:2 * D]
    g1 = mod[:, 2 * D:3 * D]
    u = (x * _rms_scale(x)) * (pre_g_ref[...] * (1.0 + sc1)) + sh1
    u_s[...] = u.astype(_BF16)

    for jb in range(D_CV // GLU_COL_BLOCK):
        c0 = jb * GLU_COL_BLOCK
        a = _dot(u_s[...], w_glu_ref[0, :, c0:c0 + GLU_COL_BLOCK])
        gt = _dot(u_s[...], w_glu_ref[0, :, c0 + D_CV:c0 + D_CV + GLU_COL_BLOCK])
        glu = a * _sigmoid(gt)
        for jj in range(GLU_COL_BLOCK // LANES):
            cv_buf[jb * (GLU_COL_BLOCK // LANES) + jj, CV_HALO:CV_HALO + T, :] = (
                glu[:, jj * LANES:(jj + 1) * LANES])
    cv_off = CV_HALO - (CV_KERNEL - 1)
    for j in range(N_CV_LT):
        for i in range(T // CV_ROWS):
            r0 = i * CV_ROWS + cv_off
            acc = cv_b_ref[j] + cv_w_ref[j, 0:1, :] * cv_buf[j, r0:r0 + CV_ROWS, :]
            for k in range(1, CV_KERNEL):
                acc = acc + cv_w_ref[j, k:k + 1, :] * cv_buf[j, r0 + k:r0 + k + CV_ROWS, :]
            y_s[j, i * CV_ROWS:(i + 1) * CV_ROWS, :] = acc
    cv_buf[:, 0:CV_HALO, :] = cv_buf[:, T:T + CV_HALO, :]

    tot = y_s[0]
    for j in range(1, N_CV_LT):
        tot = tot + y_s[j]
    mu = jnp.broadcast_to(jnp.sum(tot, axis=-1, keepdims=True) * (1.0 / D_CV), (T, LANES))
    sq = jnp.square(y_s[0] - mu)
    for j in range(1, N_CV_LT):
        sq = sq + jnp.square(y_s[j] - mu)
    var = jnp.sum(sq, axis=-1, keepdims=True) * (1.0 / D_CV)
    inv = jnp.broadcast_to(lax.rsqrt(var + EPS), (T, LANES))
    for j in range(N_CV_LT):
        lcols = slice(j * LANES, (j + 1) * LANES)
        yn = (y_s[j] - mu) * inv * ln_g_ref[:, lcols] + ln_b_ref[:, lcols]
        hy_s[:, D_ML + j * LANES:D_ML + (j + 1) * LANES] = _silu(yn).astype(_BF16)

    gates = _dot(u_s[...], w_main_ref[0, :, GATE_COL:GATE_COL + LANES]) + gate_b_ref[...]
    lane_t = lax.broadcasted_iota(jnp.int32, (T, LANES), 1)
    log_f = jnp.minimum(gates, 0.0) - jnp.log1p(jnp.exp(-jnp.abs(gates)))
    z = jnp.where(lane_t < N_HEADS, gates, log_f)
    row_i = lax.broadcasted_iota(jnp.int32, (L, L), 0)
    col_i = lax.broadcasted_iota(jnp.int32, (L, L), 1)
    causal = row_i >= col_i
    tril = causal.astype(_F32)
    lane_l = lax.broadcasted_iota(jnp.int32, (L, LANES), 1)
    zb_chunks, zbt_chunks = [], []
    for c in range(T // L):
        zc = z[c * L:(c + 1) * L, :]
        csum = jnp.dot(tril, zc, precision=lax.Precision.HIGHEST, preferred_element_type=_F32)
        zb = jnp.where(lane_l < N_HEADS, zc, csum)
        zb_chunks.append(zb)
        zbt_chunks.append(zb.T)

    off = QK_HALO - (QK_CONV - 1)

    def qk_conv(slab, col0):
        cols = slice(col0, col0 + LANES)
        acc = qkc_b_ref[:, cols] + qkc_w_ref[0:1, cols] * proj_buf[slab, off:off + T, :]
        for k in range(1, QK_CONV):
            acc = acc + qkc_w_ref[k:k + 1, cols] * proj_buf[slab, off + k:off + k + T, :]
        return _silu(acc)

    ones_blk = jnp.ones((L, LANES), _BF16)
    mix = None
    for h in range(N_HEADS):
        hc = slice(h * HEAD_DIM, (h + 1) * HEAD_DIM)
        slab_q = h * 4 * LT_PER_HEAD
        slab_k = slab_q + LT_PER_HEAD
        slab_v = slab_k + LT_PER_HEAD
        slab_o = slab_v + LT_PER_HEAD
        for g in range(4):
            c0 = g * D_ML + h * HEAD_DIM
            res = _dot(u_s[...], w_main_ref[0, :, c0:c0 + HEAD_DIM])
            for jj in range(LT_PER_HEAD):
                proj_buf[slab_q + g * LT_PER_HEAD + jj, QK_HALO:QK_HALO + T, :] = (
                    res[:, jj * LANES:(jj + 1) * LANES])
        for jj in range(LT_PER_HEAD):
            lt = h * LT_PER_HEAD + jj
            q_s[:, lt * LANES:(lt + 1) * LANES] = qk_conv(slab_q + jj, lt * LANES).astype(_BF16)
            kt_s[lt * LANES:(lt + 1) * LANES, :] = (
                qk_conv(slab_k + jj, D_ML + lt * LANES) * (HEAD_DIM ** -0.5)).T
        proj_buf[slab_q:slab_v, 0:QK_HALO, :] = proj_buf[slab_q:slab_v, T:T + QK_HALO, :]

        cn = cn_s[h]
        m_prev = m_s[h, 0:1, :]
        for c in range(T // L):
            r = slice(c * L, (c + 1) * L)
            pr = slice(QK_HALO + c * L, QK_HALO + (c + 1) * L)
            zb, zbt = zb_chunks[c], zbt_chunks[c]
            b_rep = jnp.broadcast_to(zb[:, N_HEADS + h:N_HEADS + h + 1], (L, LANES))
            i_row = zbt[h:h + 1, :]
            b_row = zbt[N_HEADS + h:N_HEADS + h + 1, :]
            q = q_s[r, hc]
            kt = kt_s[hc, r]
            v_aug = jnp.concatenate(
                [proj_buf[slab_v + jj, pr, :].astype(_BF16)
                 for jj in range(LT_PER_HEAD)] + [ones_blk], axis=1)
            dmat = jnp.where(causal, b_rep + (i_row - b_row), _NEG_INF)
            bm = b_rep + m_prev
            m_t = jnp.maximum(bm, jnp.broadcast_to(jnp.max(dmat, axis=-1, keepdims=True), (L, LANES)))
            s = _dot(q, kt.astype(_BF16)) * jnp.exp(dmat - m_t)
            intra = _dot(s.astype(_BF16), v_aug)
            inter = _dot(q, cn.astype(_BF16))
            w_inter = jnp.exp(bm - m_t)
            comb = [w_inter * inter[:, jj * LANES:(jj + 1) * LANES] + intra[:, jj * LANES:(jj + 1) * LANES]
                    for jj in range(LT_PER_HEAD + 1)]
            den = comb[LT_PER_HEAD]
            inv_den = 1.0 / jnp.maximum(jnp.abs(den), jnp.exp(-m_t))
            hh = [comb[jj] * inv_den for jj in range(LT_PER_HEAD)]
            hsq = hh[0] * hh[0]
            for jj in range(1, LT_PER_HEAD):
                hsq = hsq + hh[jj] * hh[jj]
            rs = jnp.broadcast_to(
                lax.rsqrt(jnp.sum(hsq, axis=-1, keepdims=True) * (1.0 / HEAD_DIM) + EPS), (L, LANES))
            for jj in range(LT_PER_HEAD):
                cols = slice(h * HEAD_DIM + jj * LANES, h * HEAD_DIM + (jj + 1) * LANES)
                o_pre = proj_buf[slab_o + jj, pr, :]
                hy_s[r, cols] = (hh[jj] * rs * mln_g_ref[:, cols] * _sigmoid(o_pre)).astype(_BF16)
            b_last = b_rep[L - 1:L, :]
            g_row = b_last - b_row + i_row
            m_new = jnp.maximum(
                b_last + m_prev,
                jnp.broadcast_to(jnp.max(g_row, axis=-1, keepdims=True), (1, LANES)))
            kwt = kt * jnp.exp(g_row - m_new)
            upd = _dot(kwt.astype(_BF16), v_aug)
            decay = jnp.exp(b_last + m_prev - m_new)
            cn = jnp.concatenate(
                [decay * cn[:, jj * LANES:(jj + 1) * LANES] + upd[:, jj * LANES:(jj + 1) * LANES]
                 for jj in range(LT_PER_HEAD + 1)], axis=1)
            m_prev = m_new
        cn_s[h] = cn
        m_s[h] = jnp.broadcast_to(m_prev, (SUBLANES, LANES))
        part = _dot(hy_s[:, hc], w_out_ref[0, hc, :])
        mix = part if mix is None else mix + part
    mix = mix + _dot(hy_s[:, D_ML:D_ML + D_CV], w_out_ref[0, D_ML:D_ML + D_CV, :])

    out_ref[0] = x + (mix * _rms_scale(mix)) * (g1 * post_g_ref[...])


def _const_spec(shape, index=None):
    index = index if index is not None else (0,) * len(shape)
    return pl.BlockSpec(shape, lambda b, t: index, pipeline_mode=pl.Buffered(1))


def _layer_spec(arr, l):
    return _const_spec((1,) + arr.shape[1:], (l,) + (0,) * (arr.ndim - 1))


def _mixer(l, x, mod_all, pre_g, post_g, w_main, w_glu, qkc_w, qkc_b, igate_b, fgate_b, mln_g,
           cv_w, cv_b, ln_g, ln_b, w_out):
    bsz, seq, d = x.shape
    T = SEQ_TILE
    gate_b = jnp.pad(jnp.concatenate([igate_b, fgate_b]), (0, LANES - N_GATE)).reshape(1, LANES)
    cv_w3 = jnp.pad(cv_w, ((0, 4 * SUBLANES - CV_KERNEL), (0, 0)))
    cv_w3 = cv_w3.reshape(4 * SUBLANES, N_CV_LT, LANES).transpose(1, 0, 2)
    cv_b3 = cv_b.reshape(N_CV_LT, 1, LANES)
    qkc_w8 = jnp.pad(qkc_w, ((0, SUBLANES - QK_CONV), (0, 0)))

    operands = [
        (x, pl.BlockSpec((1, T, d), lambda b, t: (b, t, 0))),
        (mod_all, pl.BlockSpec((1, 1, 1, N_MOD * d), lambda b, t: (l, b, 0, 0))),
        (pre_g.reshape(1, d), None),
        (post_g.reshape(1, d), None),
        (w_main, _layer_spec(w_main, l)),
        (w_glu, _layer_spec(w_glu, l)),
        (gate_b, None),
        (qkc_w8, None), (qkc_b.reshape(1, 2 * D_ML), None), (mln_g.reshape(1, D_ML), None),
        (cv_w3, None), (cv_b3, None), (ln_g.reshape(1, D_CV), None), (ln_b.reshape(1, D_CV), None),
        (w_out, _layer_spec(w_out, l)),
    ]
    args = [a for a, _ in operands]
    in_specs = [s if s is not None else _const_spec(a.shape) for a, s in operands]
    scratch = [
        pltpu.VMEM((T, d), _BF16),
        pltpu.VMEM((4 * D_ML // LANES, QK_HALO + T, LANES), _F32),
        pltpu.VMEM((T, D_ML), _BF16),
        pltpu.VMEM((D_ML, T), _F32),
        pltpu.VMEM((N_CV_LT, CV_HALO + T, LANES), _F32),
        pltpu.VMEM((N_CV_LT, T, LANES), _F32),
        pltpu.VMEM((T, D_ML + D_CV), _BF16),
        pltpu.VMEM((N_HEADS, HEAD_DIM, HEAD_DIM + LANES), _F32),
        pltpu.VMEM((N_HEADS, SUBLANES, LANES), _F32),
    ]
    return pl.pallas_call(
        _mixer_kernel,
        grid=(bsz, seq // T),
        in_specs=in_specs,
        out_specs=pl.BlockSpec((1, T, d), lambda b, t: (b, t, 0)),
        out_shape=jax.ShapeDtypeStruct(x.shape, x.dtype),
        scratch_shapes=scratch,
        compiler_params=pltpu.CompilerParams(
            dimension_semantics=("arbitrary", "arbitrary"),
            vmem_limit_bytes=VMEM_LIMIT_BYTES),
        name="mixer",
    )(*args)


def _ffn_kernel(x_ref, mod_ref, pre_g_ref, post_g_ref, up_ref, cw_ref, cb_ref, down_ref,
                out_ref, u_s, buf_s, halo_s, f_s):
    T = SEQ_TILE
    D = D_MODEL
    W = FFN_COL_BLOCK
    t_idx = pl.program_id(1)

    @pl.when(t_idx == 0)
    def _():
        halo_s[...] = jnp.zeros(halo_s.shape, _F32)

    mod = mod_ref[0, 0]
    sh2 = mod[:, 3 * D:4 * D]
    sc2 = mod[:, 4 * D:5 * D]
    g2 = mod[:, 5 * D:6 * D]
    pre_gain = pre_g_ref[...] * (1.0 + sc2)
    post_gain = g2 * post_g_ref[...]
    off = QK_HALO - (FFN_KERNEL - 1)
    n_lt = W // LANES

    x = x_ref[0]
    u_s[...] = ((x * _rms_scale(x)) * pre_gain + sh2).astype(_BF16)

    def up_conv(slot, part, col0):
        res = _dot(u_s[...], up_ref[0, :, col0:col0 + W])
        out = []
        for jj in range(n_lt):
            lt = col0 // LANES + jj
            cols = slice(lt * LANES, (lt + 1) * LANES)
            buf = buf_s.at[slot, part, jj]
            buf[0:QK_HALO, :] = halo_s[lt]
            buf[QK_HALO:QK_HALO + T, :] = res[:, jj * LANES:(jj + 1) * LANES]
            halo_s[lt] = buf[T:T + QK_HALO, :]
            acc = cb_ref[:, cols] + cw_ref[0:1, cols] * buf[off:off + T, :]
            for k in range(1, FFN_KERNEL):
                acc = acc + cw_ref[k:k + 1, cols] * buf[off + k:off + k + T, :]
            out.append(acc)
        return out

    for j in range(D_FF // W):
        a = up_conv(j % 2, 0, j * W)
        gt = up_conv(j % 2, 1, D_FF + j * W)
        for jj in range(n_lt):
            f_s[:, j * W + jj * LANES:j * W + (jj + 1) * LANES] = (
                _silu(gt[jj]) * a[jj]).astype(_BF16)

    f = _dot(f_s[...], down_ref[0])
    out_ref[0] = x + (f * _rms_scale(f)) * post_gain


def _ffn(l, x, mod_all, pre_g, post_g, up, conv_w, conv_b, down):
    bsz, seq, d = x.shape
    T = SEQ_TILE
    cw8 = jnp.pad(conv_w, ((0, SUBLANES - FFN_KERNEL), (0, 0)))
    operands = [
        (x, pl.BlockSpec((1, T, d), lambda b, t: (b, t, 0))),
        (mod_all, pl.BlockSpec((1, 1, 1, N_MOD * d), lambda b, t: (l, b, 0, 0))),
        (pre_g.reshape(1, d), None),
        (post_g.reshape(1, d), None),
        (up, _layer_spec(up, l)),
        (cw8, None),
        (conv_b.reshape(1, 2 * D_FF), None),
        (down, _layer_spec(down, l)),
    ]
    args = [a for a, _ in operands]
    in_specs = [s if s is not None else _const_spec(a.shape) for a, s in operands]
    scratch = [
        pltpu.VMEM((T, d), _BF16),
        pltpu.VMEM((2, 2, FFN_COL_BLOCK // LANES, QK_HALO + T, LANES), _F32),
        pltpu.VMEM((2 * D_FF // LANES, QK_HALO, LANES), _F32),
        pltpu.VMEM((T, D_FF), _BF16),
    ]
    return pl.pallas_call(
        _ffn_kernel,
        grid=(bsz, seq // T),
        in_specs=in_specs,
        out_specs=pl.BlockSpec((1, T, d), lambda b, t: (b, t, 0)),
        out_shape=jax.ShapeDtypeStruct(x.shape, x.dtype),
        scratch_shapes=scratch,
        compiler_params=pltpu.CompilerParams(
            dimension_semantics=("arbitrary", "arbitrary"),
            vmem_limit_bytes=VMEM_LIMIT_BYTES),
        name="ffn",
    )(*args)


def kernel(x, c, ada_w, ada_b, mix_pre_g, mix_post_g, w_in, qk_conv_w, qk_conv_b, igate_b, fgate_b, ml_norm_g, cv_dw_w, cv_dw_b, cv_ln_g, cv_ln_b, w_out, ffn_pre_g, ffn_post_g, ffn_up, ffn_conv_w, ffn_conv_b, ffn_down):
    bsz, seq, d = x.shape
    depth = ada_w.shape[0]
    assert d == D_MODEL and seq % SEQ_TILE == 0
    mod_all = _modulation(c, ada_w, ada_b).reshape(depth, bsz, 1, N_MOD * d)
    w_main = w_in[:, :, 0:GATE_COL + LANES].astype(_BF16)
    w_glu = w_in[:, :, GATE_COL + N_GATE:].astype(_BF16)
    w_out16 = w_out.astype(_BF16)
    up16 = ffn_up.astype(_BF16)
    down16 = ffn_down.astype(_BF16)
    for l in range(depth):
        x = _mixer(l, x, mod_all, mix_pre_g[l], mix_post_g[l], w_main, w_glu, qk_conv_w[l],
                   qk_conv_b[l], igate_b[l], fgate_b[l], ml_norm_g[l], cv_dw_w[l], cv_dw_b[l],
                   cv_ln_g[l], cv_ln_b[l], w_out16)
        x = _ffn(l, x, mod_all, ffn_pre_g[l], ffn_post_g[l], up16, ffn_conv_w[l],
                 ffn_conv_b[l], down16)
    return x
```

```python
import jax
import jax.numpy as jnp
from jax import lax
from jax.experimental import pallas as pl
from jax.experimental.pallas import tpu as pltpu

D_MODEL = 1024
N_HEADS = 4
HEAD_DIM = 256
D_ML = N_HEADS * HEAD_DIM
D_CV = 1024
QK_CONV = 4
CV_KERNEL = 31
ML_CHUNK = 128
D_FF = 2816
FFN_KERNEL = 3
N_MOD = 6
N_GATE = 2 * N_HEADS
EPS = 1e-6

LANES = 128
SUBLANES = 8
SEQ_TILE = 512
QK_HALO = SUBLANES
CV_HALO = 4 * SUBLANES
CV_ROWS = 64
FFN_COL_BLOCK = 256
VMEM_LIMIT_BYTES = 58 * 1024 * 1024

LT_PER_HEAD = HEAD_DIM // LANES
N_CV_LT = D_CV // LANES
GATE_COL = 4 * D_ML
GLU_COL_BLOCK = 512

_F32 = jnp.float32
_BF16 = jnp.bfloat16
_NEG_INF = float("-inf")


def _dot(a, b):
    return jnp.dot(a, b, preferred_element_type=_F32)


def _sigmoid(x):
    return 0.5 * jnp.tanh(0.5 * x) + 0.5


def _silu(x):
    h = 0.5 * x
    return h * jnp.tanh(h) + h


def _rms_scale(x):
    return lax.rsqrt(jnp.mean(x * x, axis=-1, keepdims=True) + EPS)


def _mod_kernel(c_ref, w_ref, b_ref, o_ref):
    cond = _silu(c_ref[...]).astype(_BF16)
    o_ref[0] = _dot(cond, w_ref[0].astype(_BF16)) + b_ref[0]


def _modulation(c, ada_w, ada_b):
    depth, d, n = ada_w.shape
    bsz = c.shape[0]
    nb = n // d
    return pl.pallas_call(
        _mod_kernel,
        grid=(depth, nb),
        in_specs=[
            pl.BlockSpec((bsz, d), lambda l, j: (0, 0)),
            pl.BlockSpec((1, d, d), lambda l, j: (l, 0, j)),
            pl.BlockSpec((1, 1, d), lambda l, j: (l, 0, j)),
        ],
        out_specs=pl.BlockSpec((1, bsz, d), lambda l, j: (l, 0, j)),
        out_shape=jax.ShapeDtypeStruct((depth, bsz, n), _F32),
        compiler_params=pltpu.CompilerParams(
            dimension_semantics=("arbitrary", "arbitrary")),
        name="modulation",
    )(c, ada_w, ada_b.reshape(depth, 1, n))


def _mixer_kernel(x_ref, mod_ref, pre_g_ref, post_g_ref, w_main_ref, w_glu_ref, gate_b_ref,
                  qkc_w_ref, qkc_b_ref, mln_g_ref, cv_w_ref, cv_b_ref, ln_g_ref, ln_b_ref,
                  w_out_ref, out_ref,
                  u_s, proj_buf, q_s, kt_s, cv_buf, y_s, hy_s, cn_s, m_s):
    T = SEQ_TILE
    D = D_MODEL
    L = ML_CHUNK
    t_idx = pl.program_id(1)

    @pl.when(t_idx == 0)
    def _():
        proj_buf[:, 0:QK_HALO, :] = jnp.zeros((proj_buf.shape[0], QK_HALO, LANES), _F32)
        cv_buf[:, 0:CV_HALO, :] = jnp.zeros((cv_buf.shape[0], CV_HALO, LANES), _F32)
        cn_s[...] = jnp.zeros(cn_s.shape, _F32)
        m_s[...] = jnp.zeros(m_s.shape, _F32)

    x = x_ref[0]
    mod = mod_ref[0, 0]
    sh1 = mod[:, 0:D]
    sc1 = mod[:, D:2 * D]
    g1 = mod[:, 2 * D:3 * D]
    u = (x * _rms_scale(x)) * (pre_g_ref[...] * (1.0 + sc1)) + sh1
    u_s[...] = u.astype(_BF16)

    for jb in range(D_CV // GLU_COL_BLOCK):
        c0 = jb * GLU_COL_BLOCK
        a = _dot(u_s[...], w_glu_ref[0, :, c0:c0 + GLU_COL_BLOCK])
        gt = _dot(u_s[...], w_glu_ref[0, :, c0 + D_CV:c0 + D_CV + GLU_COL_BLOCK])
        glu = a * _sigmoid(gt)
        for jj in range(GLU_COL_BLOCK // LANES):
            cv_buf[jb * (GLU_COL_BLOCK // LANES) + jj, CV_HALO:CV_HALO + T, :] = (
                glu[:, jj * LANES:(jj + 1) * LANES])
    cv_off = CV_HALO - (CV_KERNEL - 1)
    for j in range(N_CV_LT):
        for i in range(T // CV_ROWS):
            r0 = i * CV_ROWS + cv_off
            acc = cv_b_ref[j] + cv_w_ref[j, 0:1, :] * cv_buf[j, r0:r0 + CV_ROWS, :]
            for k in range(1, CV_KERNEL):
                acc = acc + cv_w_ref[j, k:k + 1, :] * cv_buf[j, r0 + k:r0 + k + CV_ROWS, :]
            y_s[j, i * CV_ROWS:(i + 1) * CV_ROWS, :] = acc
    cv_buf[:, 0:CV_HALO, :] = cv_buf[:, T:T + CV_HALO, :]

    tot = y_s[0]
    for j in range(1, N_CV_LT):
        tot = tot + y_s[j]
    mu = jnp.broadcast_to(jnp.sum(tot, axis=-1, keepdims=True) * (1.0 / D_CV), (T, LANES))
    sq = jnp.square(y_s[0] - mu)
    for j in range(1, N_CV_LT):
        sq = sq + jnp.square(y_s[j] - mu)
    var = jnp.sum(sq, axis=-1, keepdims=True) * (1.0 / D_CV)
    inv = jnp.broadcast_to(lax.rsqrt(var + EPS), (T, LANES))
    for j in range(N_CV_LT):
        lcols = slice(j * LANES, (j + 1) * LANES)
        yn = (y_s[j] - mu) * inv * ln_g_ref[:, lcols] + ln_b_ref[:, lcols]
        hy_s[:, D_ML + j * LANES:D_ML + (j + 1) * LANES] = _silu(yn).astype(_BF16)

    gates = _dot(u_s[...], w_main_ref[0, :, GATE_COL:GATE_COL + LANES]) + gate_b_ref[...]
    lane_t = lax.broadcasted_iota(jnp.int32, (T, LANES), 1)
    log_f = jnp.minimum(gates, 0.0) - jnp.log1p(jnp.exp(-jnp.abs(gates)))
    z = jnp.where(lane_t < N_HEADS, gates, log_f)
    row_i = lax.broadcasted_iota(jnp.int32, (L, L), 0)
    col_i = lax.broadcasted_iota(jnp.int32, (L, L), 1)
    causal = row_i >= col_i
    tril = causal.astype(_F32)
    lane_l = lax.broadcasted_iota(jnp.int32, (L, LANES), 1)
    zb_chunks, zbt_chunks = [], []
    for c in range(T // L):
        zc = z[c * L:(c + 1) * L, :]
        csum = jnp.dot(tril, zc, precision=lax.Precision.HIGHEST, preferred_element_type=_F32)
        zb = jnp.where(lane_l < N_HEADS, zc, csum)
        zb_chunks.append(zb)
        zbt_chunks.append(zb.T)

    off = QK_HALO - (QK_CONV - 1)

    def qk_conv(slab, col0):
        cols = slice(col0, col0 + LANES)
        acc = qkc_b_ref[:, cols] + qkc_w_ref[0:1, cols] * proj_buf[slab, off:off + T, :]
        for k in range(1, QK_CONV):
            acc = acc + qkc_w_ref[k:k + 1, cols] * proj_buf[slab, off + k:off + k + T, :]
        return _silu(acc)

    ones_blk = jnp.ones((L, LANES), _BF16)
    mix = None
    for h in range(N_HEADS):
        hc = slice(h * HEAD_DIM, (h + 1) * HEAD_DIM)
        slab_q = h * 4 * LT_PER_HEAD
        slab_k = slab_q + LT_PER_HEAD
        slab_v = slab_k + LT_PER_HEAD
        slab_o = slab_v + LT_PER_HEAD
        for g in range(4):
            c0 = g * D_ML + h * HEAD_DIM
            res = _dot(u_s[...], w_main_ref[0, :, c0:c0 + HEAD_DIM])
            for jj in range(LT_PER_HEAD):
                proj_buf[slab_q + g * LT_PER_HEAD + jj, QK_HALO:QK_HALO + T, :] = (
                    res[:, jj * LANES:(jj + 1) * LANES])
        for jj in range(LT_PER_HEAD):
            lt = h * LT_PER_HEAD + jj
            q_s[:, lt * LANES:(lt + 1) * LANES] = qk_conv(slab_q + jj, lt * LANES).astype(_BF16)
            kt_s[lt * LANES:(lt + 1) * LANES, :] = (
                qk_conv(slab_k + jj, D_ML + lt * LANES) * (HEAD_DIM ** -0.5)).T
        proj_buf[slab_q:slab_v, 0:QK_HALO, :] = proj_buf[slab_q:slab_v, T:T + QK_HALO, :]

        cn = cn_s[h]
        m_prev = m_s[h, 0:1, :]
        for c in range(T // L):
            r = slice(c * L, (c + 1) * L)
            pr = slice(QK_HALO + c * L, QK_HALO + (c + 1) * L)
            zb, zbt = zb_chunks[c], zbt_chunks[c]
            b_rep = jnp.broadcast_to(zb[:, N_HEADS + h:N_HEADS + h + 1], (L, LANES))
            i_row = zbt[h:h + 1, :]
            b_row = zbt[N_HEADS + h:N_HEADS + h + 1, :]
            q = q_s[r, hc]
            kt = kt_s[hc, r]
            v_aug = jnp.concatenate(
                [proj_buf[slab_v + jj, pr, :].astype(_BF16)
                 for jj in range(LT_PER_HEAD)] + [ones_blk], axis=1)
            dmat = jnp.where(causal, b_rep + (i_row - b_row), _NEG_INF)
            bm = b_rep + m_prev
            m_t = jnp.maximum(bm, jnp.broadcast_to(jnp.max(dmat, axis=-1, keepdims=True), (L, LANES)))
            s = _dot(q, kt.astype(_BF16)) * jnp.exp(dmat - m_t)
            intra = _dot(s.astype(_BF16), v_aug)
            inter = _dot(q, cn.astype(_BF16))
            w_inter = jnp.exp(bm - m_t)
            comb = [w_inter * inter[:, jj * LANES:(jj + 1) * LANES] + intra[:, jj * LANES:(jj + 1) * LANES]
                    for jj in range(LT_PER_HEAD + 1)]
            den = comb[LT_PER_HEAD]
            inv_den = 1.0 / jnp.maximum(jnp.abs(den), jnp.exp(-m_t))
            hh = [comb[jj] * inv_den for jj in range(LT_PER_HEAD)]
            hsq = hh[0] * hh[0]
            for jj in range(1, LT_PER_HEAD):
                hsq = hsq + hh[jj] * hh[jj]
            rs = jnp.broadcast_to(
                lax.rsqrt(jnp.sum(hsq, axis=-1, keepdims=True) * (1.0 / HEAD_DIM) + EPS), (L, LANES))
            for jj in range(LT_PER_HEAD):
                cols = slice(h * HEAD_DIM + jj * LANES, h * HEAD_DIM + (jj + 1) * LANES)
                o_pre = proj_buf[slab_o + jj, pr, :]
                hy_s[r, cols] = (hh[jj] * rs * mln_g_ref[:, cols] * _sigmoid(o_pre)).astype(_BF16)
            b_last = b_rep[L - 1:L, :]
            g_row = b_last - b_row + i_row
            m_new = jnp.maximum(
                b_last + m_prev,
                jnp.broadcast_to(jnp.max(g_row, axis=-1, keepdims=True), (1, LANES)))
            kwt = kt * jnp.exp(g_row - m_new)
            upd = _dot(kwt.astype(_BF16), v_aug)
            decay = jnp.exp(b_last + m_prev - m_new)
            cn = jnp.concatenate(
                [decay * cn[:, jj * LANES:(jj + 1) * LANES] + upd[:, jj * LANES:(jj + 1) * LANES]
                 for jj in range(LT_PER_HEAD + 1)], axis=1)
            m_prev = m_new
        cn_s[h] = cn
        m_s[h] = jnp.broadcast_to(m_prev, (SUBLANES, LANES))
        part = _dot(hy_s[:, hc], w_out_ref[0, hc, :])
        mix = part if mix is None else mix + part
    mix = mix + _dot(hy_s[:, D_ML:D_ML + D_CV], w_out_ref[0, D_ML:D_ML + D_CV, :])

    out_ref[0] = x + (mix * _rms_scale(mix)) * (g1 * post_g_ref[...])


def _const_spec(shape, index=None):
    index = index if index is not None else (0,) * len(shape)
    return pl.BlockSpec(shape, lambda b, t: index, pipeline_mode=pl.Buffered(1))


def _layer_spec(arr, l):
    return _const_spec((1,) + arr.shape[1:], (l,) + (0,) * (arr.ndim - 1))


def _mixer(l, x, mod_all, pre_g, post_g, w_main, w_glu, qkc_w, qkc_b, igate_b, fgate_b, mln_g,
           cv_w, cv_b, ln_g, ln_b, w_out):
    bsz, seq, d = x.shape
    T = SEQ_TILE
    gate_b = jnp.pad(jnp.concatenate([igate_b, fgate_b]), (0, LANES - N_GATE)).reshape(1, LANES)
    cv_w3 = jnp.pad(cv_w, ((0, 4 * SUBLANES - CV_KERNEL), (0, 0)))
    cv_w3 = cv_w3.reshape(4 * SUBLANES, N_CV_LT, LANES).transpose(1, 0, 2)
    cv_b3 = cv_b.reshape(N_CV_LT, 1, LANES)
    qkc_w8 = jnp.pad(qkc_w, ((0, SUBLANES - QK_CONV), (0, 0)))

    operands = [
        (x, pl.BlockSpec((1, T, d), lambda b, t: (b, t, 0))),
        (mod_all, pl.BlockSpec((1, 1, 1, N_MOD * d), lambda b, t: (l, b, 0, 0))),
        (pre_g.reshape(1, d), None),
        (post_g.reshape(1, d), None),
        (w_main, _layer_spec(w_main, l)),
        (w_glu, _layer_spec(w_glu, l)),
        (gate_b, None),
        (qkc_w8, None), (qkc_b.reshape(1, 2 * D_ML), None), (mln_g.reshape(1, D_ML), None),
        (cv_w3, None), (cv_b3, None), (ln_g.reshape(1, D_CV), None), (ln_b.reshape(1, D_CV), None),
        (w_out, _layer_spec(w_out, l)),
    ]
    args = [a for a, _ in operands]
    in_specs = [s if s is not None else _const_spec(a.shape) for a, s in operands]
    scratch = [
        pltpu.VMEM((T, d), _BF16),
        pltpu.VMEM((4 * D_ML // LANES, QK_HALO + T, LANES), _F32),
        pltpu.VMEM((T, D_ML), _BF16),
        pltpu.VMEM((D_ML, T), _F32),
        pltpu.VMEM((N_CV_LT, CV_HALO + T, LANES), _F32),
        pltpu.VMEM((N_CV_LT, T, LANES), _F32),
        pltpu.VMEM((T, D_ML + D_CV), _BF16),
        pltpu.VMEM((N_HEADS, HEAD_DIM, HEAD_DIM + LANES), _F32),
        pltpu.VMEM((N_HEADS, SUBLANES, LANES), _F32),
    ]
    return pl.pallas_call(
        _mixer_kernel,
        grid=(bsz, seq // T),
        in_specs=in_specs,
        out_specs=pl.BlockSpec((1, T, d), lambda b, t: (b, t, 0)),
        out_shape=jax.ShapeDtypeStruct(x.shape, x.dtype),
        scratch_shapes=scratch,
        compiler_params=pltpu.CompilerParams(
            dimension_semantics=("arbitrary", "arbitrary"),
            vmem_limit_bytes=VMEM_LIMIT_BYTES),
        name="mixer",
    )(*args)


def _ffn_kernel(x_ref, mod_ref, pre_g_ref, post_g_ref, up_ref, cw_ref, cb_ref, down_ref,
                out_ref, u_s, buf_s, halo_s, f_s):
    T = SEQ_TILE
    D = D_MODEL
    W = FFN_COL_BLOCK
    t_idx = pl.program_id(1)

    @pl.when(t_idx == 0)
    def _():
        halo_s[...] = jnp.zeros(halo_s.shape, _F32)

    mod = mod_ref[0, 0]
    sh2 = mod[:, 3 * D:4 * D]
    sc2 = mod[:, 4 * D:5 * D]
    g2 = mod[:, 5 * D:6 * D]
    pre_gain = pre_g_ref[...] * (1.0 + sc2)
    post_gain = g2 * post_g_ref[...]
    off = QK_HALO - (FFN_KERNEL - 1)
    n_lt = W // LANES

    x = x_ref[0]
    u_s[...] = ((x * _rms_scale(x)) * pre_gain + sh2).astype(_BF16)

    def up_conv(slot, part, col0):
        res = _dot(u_s[...], up_ref[0, :, col0:col0 + W])
        out = []
        for jj in range(n_lt):
            lt = col0 // LANES + jj
            cols = slice(lt * LANES, (lt + 1) * LANES)
            buf = buf_s.at[slot, part, jj]
            buf[0:QK_HALO, :] = halo_s[lt]
            buf[QK_HALO:QK_HALO + T, :] = res[:, jj * LANES:(jj + 1) * LANES]
            halo_s[lt] = buf[T:T + QK_HALO, :]
            acc = cb_ref[:, cols] + cw_ref[0:1, cols] * buf[off:off + T, :]
            for k in range(1, FFN_KERNEL):
                acc = acc + cw_ref[k:k + 1, cols] * buf[off + k:off + k + T, :]
            out.append(acc)
        return out

    for j in range(D_FF // W):
        a = up_conv(j % 2, 0, j * W)
        gt = up_conv(j % 2, 1, D_FF + j * W)
        for jj in range(n_lt):
            f_s[:, j * W + jj * LANES:j * W + (jj + 1) * LANES] = (
                _silu(gt[jj]) * a[jj]).astype(_BF16)

    f = _dot(f_s[...], down_ref[0])
    out_ref[0] = x + (f * _rms_scale(f)) * post_gain


def _ffn(l, x, mod_all, pre_g, post_g, up, conv_w, conv_b, down):
    bsz, seq, d = x.shape
    T = SEQ_TILE
    cw8 = jnp.pad(conv_w, ((0, SUBLANES - FFN_KERNEL), (0, 0)))
    operands = [
        (x, pl.BlockSpec((1, T, d), lambda b, t: (b, t, 0))),
        (mod_all, pl.BlockSpec((1, 1, 1, N_MOD * d), lambda b, t: (l, b, 0, 0))),
        (pre_g.reshape(1, d), None),
        (post_g.reshape(1, d), None),
        (up, _layer_spec(up, l)),
        (cw8, None),
        (conv_b.reshape(1, 2 * D_FF), None),
        (down, _layer_spec(down, l)),
    ]
    args = [a for a, _ in operands]
    in_specs = [s if s is not None else _const_spec(a.shape) for a, s in operands]
    scratch = [
        pltpu.VMEM((T, d), _BF16),
        pltpu.VMEM((2, 2, FFN_COL_BLOCK // LANES, QK_HALO + T, LANES), _F32),
        pltpu.VMEM((2 * D_FF // LANES, QK_HALO, LANES), _F32),
        pltpu.VMEM((T, D_FF), _BF16),
    ]
    return pl.pallas_call(
        _ffn_kernel,
        grid=(bsz, seq // T),
        in_specs=in_specs,
        out_specs=pl.BlockSpec((1, T, d), lambda b, t: (b, t, 0)),
        out_shape=jax.ShapeDtypeStruct(x.shape, x.dtype),
        scratch_shapes=scratch,
        compiler_params=pltpu.CompilerParams(
            dimension_semantics=("arbitrary", "arbitrary"),
            vmem_limit_bytes=VMEM_LIMIT_BYTES),
        name="ffn",
    )(*args)


def kernel(x, c, ada_w, ada_b, mix_pre_g, mix_post_g, w_in, qk_conv_w, qk_conv_b, igate_b, fgate_b, ml_norm_g, cv_dw_w, cv_dw_b, cv_ln_g, cv_ln_b, w_out, ffn_pre_g, ffn_post_g, ffn_up, ffn_conv_w, ffn_conv_b, ffn_down):
    bsz, seq, d = x.shape
    depth = ada_w.shape[0]
    assert d == D_MODEL and seq % SEQ_TILE == 0
    mod_all = _modulation(c, ada_w, ada_b).reshape(depth, bsz, 1, N_MOD * d)
    w_main = w_in.astype(_BF16)
    w_glu = w_main[:, :, GATE_COL + N_GATE:]
    w_out16 = w_out.astype(_BF16)
    up16 = ffn_up.astype(_BF16)
    down16 = ffn_down.astype(_BF16)
    for l in range(depth):
        x = _mixer(l, x, mod_all, mix_pre_g[l], mix_post_g[l], w_main, w_glu, qk_conv_w[l],
                   qk_conv_b[l], igate_b[l], fgate_b[l], ml_norm_g[l], cv_dw_w[l], cv_dw_b[l],
                   cv_ln_g[l], cv_ln_b[l], w_out16)
        x = _ffn(l, x, mod_all, ffn_pre_g[l], ffn_post_g[l], up16, ffn_conv_w[l],
                 ffn_conv_b[l], down16)
    return x
```

```python
import jax
import jax.numpy as jnp
from jax import lax
from jax.experimental import pallas as pl
from jax.experimental.pallas import tpu as pltpu

D_MODEL = 1024
N_HEADS = 4
HEAD_DIM = 256
D_ML = N_HEADS * HEAD_DIM
D_CV = 1024
QK_CONV = 4
CV_KERNEL = 31
ML_CHUNK = 128
D_FF = 2816
FFN_KERNEL = 3
N_MOD = 6
N_GATE = 2 * N_HEADS
EPS = 1e-6

LANES = 128
SUBLANES = 8
SEQ_TILE = 512
QK_HALO = SUBLANES
CV_HALO = 4 * SUBLANES
CV_ROWS = 32
FFN_COL_BLOCK = 256
VMEM_LIMIT_BYTES = 58 * 1024 * 1024

LT_PER_HEAD = HEAD_DIM // LANES
N_CV_LT = D_CV // LANES
GATE_COL = 4 * D_ML
GLU_COL_BLOCK = 512

_F32 = jnp.float32
_BF16 = jnp.bfloat16
_NEG_INF = float("-inf")


def _dot(a, b):
    return jnp.dot(a, b, preferred_element_type=_F32)


def _sigmoid(x):
    return 0.5 * jnp.tanh(0.5 * x) + 0.5


def _silu(x):
    h = 0.5 * x
    return h * jnp.tanh(h) + h


def _rms_scale(x):
    return lax.rsqrt(jnp.mean(x * x, axis=-1, keepdims=True) + EPS)


def _mod_kernel(c_ref, w_ref, b_ref, o_ref):
    cond = _silu(c_ref[...]).astype(_BF16)
    o_ref[0] = _dot(cond, w_ref[0].astype(_BF16)) + b_ref[0]


def _modulation(c, ada_w, ada_b):
    depth, d, n = ada_w.shape
    bsz = c.shape[0]
    nb = n // d
    return pl.pallas_call(
        _mod_kernel,
        grid=(depth, nb),
        in_specs=[
            pl.BlockSpec((bsz, d), lambda l, j: (0, 0)),
            pl.BlockSpec((1, d, d), lambda l, j: (l, 0, j)),
            pl.BlockSpec((1, 1, d), lambda l, j: (l, 0, j)),
        ],
        out_specs=pl.BlockSpec((1, bsz, d), lambda l, j: (l, 0, j)),
        out_shape=jax.ShapeDtypeStruct((depth, bsz, n), _F32),
        compiler_params=pltpu.CompilerParams(
            dimension_semantics=("arbitrary", "arbitrary")),
        name="modulation",
    )(c, ada_w, ada_b.reshape(depth, 1, n))


def _mixer_kernel(x_ref, mod_ref, pre_g_ref, post_g_ref, w_main_ref, w_glu_ref, gate_b_ref,
                  qkc_w_ref, qkc_b_ref, mln_g_ref, cv_w_ref, cv_b_ref, ln_g_ref, ln_b_ref,
                  w_out_ref, out_ref,
                  u_s, proj_buf, q_s, kt_s, cv_buf, y_s, hy_s, cn_s, m_s):
    T = SEQ_TILE
    D = D_MODEL
    L = ML_CHUNK
    t_idx = pl.program_id(1)

    @pl.when(t_idx == 0)
    def _():
        proj_buf[:, 0:QK_HALO, :] = jnp.zeros((proj_buf.shape[0], QK_HALO, LANES), _F32)
        cv_buf[:, 0:CV_HALO, :] = jnp.zeros((cv_buf.shape[0], CV_HALO, LANES), _F32)
        cn_s[...] = jnp.zeros(cn_s.shape, _F32)
        m_s[...] = jnp.zeros(m_s.shape, _F32)

    x = x_ref[0]
    mod = mod_ref[0, 0]
    sh1 = mod[:, 0:D]
    sc1 = mod[:, D:2 * D]
    g1 = mod[:, 2 * D:3 * D]
    u = (x * _rms_scale(x)) * (pre_g_ref[...] * (1.0 + sc1)) + sh1
    u_s[...] = u.astype(_BF16)

    for jb in range(D_CV // GLU_COL_BLOCK):
        c0 = jb * GLU_COL_BLOCK
        a = _dot(u_s[...], w_glu_ref[0, :, c0:c0 + GLU_COL_BLOCK])
        gt = _dot(u_s[...], w_glu_ref[0, :, c0 + D_CV:c0 + D_CV + GLU_COL_BLOCK])
        glu = a * _sigmoid(gt)
        for jj in range(GLU_COL_BLOCK // LANES):
            cv_buf[jb * (GLU_COL_BLOCK // LANES) + jj, CV_HALO:CV_HALO + T, :] = (
                glu[:, jj * LANES:(jj + 1) * LANES])
    cv_off = CV_HALO - (CV_KERNEL - 1)
    for j in range(N_CV_LT):
        for i in range(T // CV_ROWS):
            r0 = i * CV_ROWS + cv_off
            acc = cv_b_ref[j] + cv_w_ref[j, 0:1, :] * cv_buf[j, r0:r0 + CV_ROWS, :]
            for k in range(1, CV_KERNEL):
                acc = acc + cv_w_ref[j, k:k + 1, :] * cv_buf[j, r0 + k:r0 + k + CV_ROWS, :]
            y_s[j, i * CV_ROWS:(i + 1) * CV_ROWS, :] = acc
    cv_buf[:, 0:CV_HALO, :] = cv_buf[:, T:T + CV_HALO, :]

    tot = y_s[0]
    for j in range(1, N_CV_LT):
        tot = tot + y_s[j]
    mu = jnp.broadcast_to(jnp.sum(tot, axis=-1, keepdims=True) * (1.0 / D_CV), (T, LANES))
    sq = jnp.square(y_s[0] - mu)
    for j in range(1, N_CV_LT):
        sq = sq + jnp.square(y_s[j] - mu)
    var = jnp.sum(sq, axis=-1, keepdims=True) * (1.0 / D_CV)
    inv = jnp.broadcast_to(lax.rsqrt(var + EPS), (T, LANES))
    for j in range(N_CV_LT):
        lcols = slice(j * LANES, (j + 1) * LANES)
        yn = (y_s[j] - mu) * inv * ln_g_ref[:, lcols] + ln_b_ref[:, lcols]
        hy_s[:, D_ML + j * LANES:D_ML + (j + 1) * LANES] = _silu(yn).astype(_BF16)

    gates = _dot(u_s[...], w_main_ref[0, :, GATE_COL:GATE_COL + LANES]) + gate_b_ref[...]
    lane_t = lax.broadcasted_iota(jnp.int32, (T, LANES), 1)
    log_f = jnp.minimum(gates, 0.0) - jnp.log1p(jnp.exp(-jnp.abs(gates)))
    z = jnp.where(lane_t < N_HEADS, gates, log_f)
    row_i = lax.broadcasted_iota(jnp.int32, (L, L), 0)
    col_i = lax.broadcasted_iota(jnp.int32, (L, L), 1)
    causal = row_i >= col_i
    tril = causal.astype(_F32)
    lane_l = lax.broadcasted_iota(jnp.int32, (L, LANES), 1)
    zb_chunks, zbt_chunks = [], []
    for c in range(T // L):
        zc = z[c * L:(c + 1) * L, :]
        csum = jnp.dot(tril, zc, precision=lax.Precision.HIGHEST, preferred_element_type=_F32)
        zb = jnp.where(lane_l < N_HEADS, zc, csum)
        zb_chunks.append(zb)
        zbt_chunks.append(zb.T)

    off = QK_HALO - (QK_CONV - 1)

    def qk_conv(slab, col0):
        cols = slice(col0, col0 + LANES)
        acc = qkc_b_ref[:, cols] + qkc_w_ref[0:1, cols] * proj_buf[slab, off:off + T, :]
        for k in range(1, QK_CONV):
            acc = acc + qkc_w_ref[k:k + 1, cols] * proj_buf[slab, off + k:off + k + T, :]
        return _silu(acc)

    ones_blk = jnp.ones((L, LANES), _BF16)
    mix = None
    for h in range(N_HEADS):
        hc = slice(h * HEAD_DIM, (h + 1) * HEAD_DIM)
        slab_q = h * 4 * LT_PER_HEAD
        slab_k = slab_q + LT_PER_HEAD
        slab_v = slab_k + LT_PER_HEAD
        slab_o = slab_v + LT_PER_HEAD
        for g in range(4):
            c0 = g * D_ML + h * HEAD_DIM
            res = _dot(u_s[...], w_main_ref[0, :, c0:c0 + HEAD_DIM])
            for jj in range(LT_PER_HEAD):
                proj_buf[slab_q + g * LT_PER_HEAD + jj, QK_HALO:QK_HALO + T, :] = (
                    res[:, jj * LANES:(jj + 1) * LANES])
        for jj in range(LT_PER_HEAD):
            lt = h * LT_PER_HEAD + jj
            q_s[:, lt * LANES:(lt + 1) * LANES] = qk_conv(slab_q + jj, lt * LANES).astype(_BF16)
            kt_s[lt * LANES:(lt + 1) * LANES, :] = (
                qk_conv(slab_k + jj, D_ML + lt * LANES) * (HEAD_DIM ** -0.5)).T
        proj_buf[slab_q:slab_v, 0:QK_HALO, :] = proj_buf[slab_q:slab_v, T:T + QK_HALO, :]

        cn = cn_s[h]
        m_prev = m_s[h, 0:1, :]
        for c in range(T // L):
            r = slice(c * L, (c + 1) * L)
            pr = slice(QK_HALO + c * L, QK_HALO + (c + 1) * L)
            zb, zbt = zb_chunks[c], zbt_chunks[c]
            b_rep = jnp.broadcast_to(zb[:, N_HEADS + h:N_HEADS + h + 1], (L, LANES))
            i_row = zbt[h:h + 1, :]
            b_row = zbt[N_HEADS + h:N_HEADS + h + 1, :]
            q = q_s[r, hc]
            kt = kt_s[hc, r]
            v_aug = jnp.concatenate(
                [proj_buf[slab_v + jj, pr, :].astype(_BF16)
                 for jj in range(LT_PER_HEAD)] + [ones_blk], axis=1)
            dmat = jnp.where(causal, b_rep + (i_row - b_row), _NEG_INF)
            bm = b_rep + m_prev
            m_t = jnp.maximum(bm, jnp.broadcast_to(jnp.max(dmat, axis=-1, keepdims=True), (L, LANES)))
            s = _dot(q, kt.astype(_BF16)) * jnp.exp(dmat - m_t)
            intra = _dot(s.astype(_BF16), v_aug)
            inter = _dot(q, cn.astype(_BF16))
            w_inter = jnp.exp(bm - m_t)
            comb = [w_inter * inter[:, jj * LANES:(jj + 1) * LANES] + intra[:, jj * LANES:(jj + 1) * LANES]
                    for jj in range(LT_PER_HEAD + 1)]
            den = comb[LT_PER_HEAD]
            inv_den = 1.0 / jnp.maximum(jnp.abs(den), jnp.exp(-m_t))
            hh = [comb[jj] * inv_den for jj in range(LT_PER_HEAD)]
            hsq = hh[0] * hh[0]
            for jj in range(1, LT_PER_HEAD):
                hsq = hsq + hh[jj] * hh[jj]
            rs = jnp.broadcast_to(
                lax.rsqrt(jnp.sum(hsq, axis=-1, keepdims=True) * (1.0 / HEAD_DIM) + EPS), (L, LANES))
            for jj in range(LT_PER_HEAD):
                cols = slice(h * HEAD_DIM + jj * LANES, h * HEAD_DIM + (jj + 1) * LANES)
                o_pre = proj_buf[slab_o + jj, pr, :]
                hy_s[r, cols] = (hh[jj] * rs * mln_g_ref[:, cols] * _sigmoid(o_pre)).astype(_BF16)
            b_last = b_rep[L - 1:L, :]
            g_row = b_last - b_row + i_row
            m_new = jnp.maximum(
                b_last + m_prev,
                jnp.broadcast_to(jnp.max(g_row, axis=-1, keepdims=True), (1, LANES)))
            kwt = kt * jnp.exp(g_row - m_new)
            upd = _dot(kwt.astype(_BF16), v_aug)
            decay = jnp.exp(b_last + m_prev - m_new)
            cn = jnp.concatenate(
                [decay * cn[:, jj * LANES:(jj + 1) * LANES] + upd[:, jj * LANES:(jj + 1) * LANES]
                 for jj in range(LT_PER_HEAD + 1)], axis=1)
            m_prev = m_new
        cn_s[h] = cn
        m_s[h] = jnp.broadcast_to(m_prev, (SUBLANES, LANES))
        part = _dot(hy_s[:, hc], w_out_ref[0, hc, :])
        mix = part if mix is None else mix + part
    mix = mix + _dot(hy_s[:, D_ML:D_ML + D_CV], w_out_ref[0, D_ML:D_ML + D_CV, :])

    out_ref[0] = x + (mix * _rms_scale(mix)) * (g1 * post_g_ref[...])


def _const_spec(shape, index=None):
    index = index if index is not None else (0,) * len(shape)
    return pl.BlockSpec(shape, lambda b, t: index, pipeline_mode=pl.Buffered(1))


def _layer_spec(arr, l):
    return _const_spec((1,) + arr.shape[1:], (l,) + (0,) * (arr.ndim - 1))


def _mixer(l, x, mod_all, pre_g, post_g, w_main, w_glu, qkc_w, qkc_b, igate_b, fgate_b, mln_g,
           cv_w, cv_b, ln_g, ln_b, w_out):
    bsz, seq, d = x.shape
    T = SEQ_TILE
    gate_b = jnp.pad(jnp.concatenate([igate_b, fgate_b]), (0, LANES - N_GATE)).reshape(1, LANES)
    cv_w3 = jnp.pad(cv_w, ((0, 4 * SUBLANES - CV_KERNEL), (0, 0)))
    cv_w3 = cv_w3.reshape(4 * SUBLANES, N_CV_LT, LANES).transpose(1, 0, 2)
    cv_b3 = cv_b.reshape(N_CV_LT, 1, LANES)
    qkc_w8 = jnp.pad(qkc_w, ((0, SUBLANES - QK_CONV), (0, 0)))

    operands = [
        (x, pl.BlockSpec((1, T, d), lambda b, t: (b, t, 0))),
        (mod_all, pl.BlockSpec((1, 1, 1, N_MOD * d), lambda b, t: (l, b, 0, 0))),
        (pre_g.reshape(1, d), None),
        (post_g.reshape(1, d), None),
        (w_main, _layer_spec(w_main, l)),
        (w_glu, _layer_spec(w_glu, l)),
        (gate_b, None),
        (qkc_w8, None), (qkc_b.reshape(1, 2 * D_ML), None), (mln_g.reshape(1, D_ML), None),
        (cv_w3, None), (cv_b3, None), (ln_g.reshape(1, D_CV), None), (ln_b.reshape(1, D_CV), None),
        (w_out, _layer_spec(w_out, l)),
    ]
    args = [a for a, _ in operands]
    in_specs = [s if s is not None else _const_spec(a.shape) for a, s in operands]
    scratch = [
        pltpu.VMEM((T, d), _BF16),
        pltpu.VMEM((4 * D_ML // LANES, QK_HALO + T, LANES), _F32),
        pltpu.VMEM((T, D_ML), _BF16),
        pltpu.VMEM((D_ML, T), _F32),
        pltpu.VMEM((N_CV_LT, CV_HALO + T, LANES), _F32),
        pltpu.VMEM((N_CV_LT, T, LANES), _F32),
        pltpu.VMEM((T, D_ML + D_CV), _BF16),
        pltpu.VMEM((N_HEADS, HEAD_DIM, HEAD_DIM + LANES), _F32),
        pltpu.VMEM((N_HEADS, SUBLANES, LANES), _F32),
    ]
    return pl.pallas_call(
        _mixer_kernel,
        grid=(bsz, seq // T),
        in_specs=in_specs,
        out_specs=pl.BlockSpec((1, T, d), lambda b, t: (b, t, 0)),
        out_shape=jax.ShapeDtypeStruct(x.shape, x.dtype),
        scratch_shapes=scratch,
        compiler_params=pltpu.CompilerParams(
            dimension_semantics=("arbitrary", "arbitrary"),
            vmem_limit_bytes=VMEM_LIMIT_BYTES),
        name="mixer",
    )(*args)


def _ffn_kernel(x_ref, mod_ref, pre_g_ref, post_g_ref, up_ref, cw_ref, cb_ref, down_ref,
                out_ref, u_s, buf_s, halo_s, f_s):
    T = SEQ_TILE
    D = D_MODEL
    W = FFN_COL_BLOCK
    t_idx = pl.program_id(1)

    @pl.when(t_idx == 0)
    def _():
        halo_s[...] = jnp.zeros(halo_s.shape, _F32)

    mod = mod_ref[0, 0]
    sh2 = mod[:, 3 * D:4 * D]
    sc2 = mod[:, 4 * D:5 * D]
    g2 = mod[:, 5 * D:6 * D]
    pre_gain = pre_g_ref[...] * (1.0 + sc2)
    post_gain = g2 * post_g_ref[...]
    off = QK_HALO - (FFN_KERNEL - 1)
    n_lt = W // LANES

    x = x_ref[0]
    u_s[...] = ((x * _rms_scale(x)) * pre_gain + sh2).astype(_BF16)

    def up_conv(slot, part, col0):
        res = _dot(u_s[...], up_ref[0, :, col0:col0 + W])
        out = []
        for jj in range(n_lt):
            lt = col0 // LANES + jj
            cols = slice(lt * LANES, (lt + 1) * LANES)
            buf = buf_s.at[slot, part, jj]
            buf[0:QK_HALO, :] = halo_s[lt]
            buf[QK_HALO:QK_HALO + T, :] = res[:, jj * LANES:(jj + 1) * LANES]
            halo_s[lt] = buf[T:T + QK_HALO, :]
            acc = cb_ref[:, cols] + cw_ref[0:1, cols] * buf[off:off + T, :]
            for k in range(1, FFN_KERNEL):
                acc = acc + cw_ref[k:k + 1, cols] * buf[off + k:off + k + T, :]
            out.append(acc)
        return out

    for j in range(D_FF // W):
        a = up_conv(j % 2, 0, j * W)
        gt = up_conv(j % 2, 1, D_FF + j * W)
        for jj in range(n_lt):
            f_s[:, j * W + jj * LANES:j * W + (jj + 1) * LANES] = (
                _silu(gt[jj]) * a[jj]).astype(_BF16)

    f = _dot(f_s[...], down_ref[0])
    out_ref[0] = x + (f * _rms_scale(f)) * post_gain


def _ffn(l, x, mod_all, pre_g, post_g, up, conv_w, conv_b, down):
    bsz, seq, d = x.shape
    T = SEQ_TILE
    cw8 = jnp.pad(conv_w, ((0, SUBLANES - FFN_KERNEL), (0, 0)))
    operands = [
        (x, pl.BlockSpec((1, T, d), lambda b, t: (b, t, 0))),
        (mod_all, pl.BlockSpec((1, 1, 1, N_MOD * d), lambda b, t: (l, b, 0, 0))),
        (pre_g.reshape(1, d), None),
        (post_g.reshape(1, d), None),
        (up, _layer_spec(up, l)),
        (cw8, None),
        (conv_b.reshape(1, 2 * D_FF), None),
        (down, _layer_spec(down, l)),
    ]
    args = [a for a, _ in operands]
    in_specs = [s if s is not None else _const_spec(a.shape) for a, s in operands]
    scratch = [
        pltpu.VMEM((T, d), _BF16),
        pltpu.VMEM((2, 2, FFN_COL_BLOCK // LANES, QK_HALO + T, LANES), _F32),
        pltpu.VMEM((2 * D_FF // LANES, QK_HALO, LANES), _F32),
        pltpu.VMEM((T, D_FF), _BF16),
    ]
    return pl.pallas_call(
        _ffn_kernel,
        grid=(bsz, seq // T),
        in_specs=in_specs,
        out_specs=pl.BlockSpec((1, T, d), lambda b, t: (b, t, 0)),
        out_shape=jax.ShapeDtypeStruct(x.shape, x.dtype),
        scratch_shapes=scratch,
        compiler_params=pltpu.CompilerParams(
            dimension_semantics=("arbitrary", "arbitrary"),
            vmem_limit_bytes=VMEM_LIMIT_BYTES),
        name="ffn",
    )(*args)


def kernel(x, c, ada_w, ada_b, mix_pre_g, mix_post_g, w_in, qk_conv_w, qk_conv_b, igate_b, fgate_b, ml_norm_g, cv_dw_w, cv_dw_b, cv_ln_g, cv_ln_b, w_out, ffn_pre_g, ffn_post_g, ffn_up, ffn_conv_w, ffn_conv_b, ffn_down):
    bsz, seq, d = x.shape
    depth = ada_w.shape[0]
    assert d == D_MODEL and seq % SEQ_TILE == 0
    mod_all = _modulation(c, ada_w, ada_b).reshape(depth, bsz, 1, N_MOD * d)
    w_main = w_in.astype(_BF16)
    w_glu = w_main[:, :, GATE_COL + N_GATE:]
    w_out16 = w_out.astype(_BF16)
    up16 = ffn_up.astype(_BF16)
    down16 = ffn_down.astype(_BF16)
    for l in range(depth):
        x = _mixer(l, x, mod_all, mix_pre_g[l], mix_post_g[l], w_main, w_glu, qk_conv_w[l],
                   qk_conv_b[l], igate_b[l], fgate_b[l], ml_norm_g[l], cv_dw_w[l], cv_dw_b[l],
                   cv_ln_g[l], cv_ln_b[l], w_out16)
        x = _ffn(l, x, mod_all, ffn_pre_g[l], ffn_post_g[l], up16, ffn_conv_w[l],
                 ffn_conv_b[l], down16)
    return x
```

```python
import jax
import jax.numpy as jnp
from jax import lax
from jax.experimental import pallas as pl
from jax.experimental.pallas import tpu as pltpu

D_MODEL = 1024
N_HEADS = 4
HEAD_DIM = 256
D_ML = N_HEADS * HEAD_DIM
D_CV = 1024
QK_CONV = 4
CV_KERNEL = 31
ML_CHUNK = 128
D_FF = 2816
FFN_KERNEL = 3
N_MOD = 6
N_GATE = 2 * N_HEADS
EPS = 1e-6

LANES = 128
SUBLANES = 8
SEQ_TILE = 512
QK_HALO = SUBLANES
CV_HALO = 4 * SUBLANES
CV_ROWS = 16
FFN_COL_BLOCK = 256
VMEM_LIMIT_BYTES = 58 * 1024 * 1024

LT_PER_HEAD = HEAD_DIM // LANES
N_CV_LT = D_CV // LANES
GATE_COL = 4 * D_ML
GLU_COL_BLOCK = 512

_F32 = jnp.float32
_BF16 = jnp.bfloat16
_NEG_INF = float("-inf")


def _dot(a, b):
    return jnp.dot(a, b, preferred_element_type=_F32)


def _sigmoid(x):
    return 0.5 * jnp.tanh(0.5 * x) + 0.5


def _silu(x):
    h = 0.5 * x
    return h * jnp.tanh(h) + h


def _rms_scale(x):
    return lax.rsqrt(jnp.mean(x * x, axis=-1, keepdims=True) + EPS)


def _mod_kernel(c_ref, w_ref, b_ref, o_ref):
    cond = _silu(c_ref[...]).astype(_BF16)
    o_ref[0] = _dot(cond, w_ref[0].astype(_BF16)) + b_ref[0]


def _modulation(c, ada_w, ada_b):
    depth, d, n = ada_w.shape
    bsz = c.shape[0]
    nb = n // d
    return pl.pallas_call(
        _mod_kernel,
        grid=(depth, nb),
        in_specs=[
            pl.BlockSpec((bsz, d), lambda l, j: (0, 0)),
            pl.BlockSpec((1, d, d), lambda l, j: (l, 0, j)),
            pl.BlockSpec((1, 1, d), lambda l, j: (l, 0, j)),
        ],
        out_specs=pl.BlockSpec((1, bsz, d), lambda l, j: (l, 0, j)),
        out_shape=jax.ShapeDtypeStruct((depth, bsz, n), _F32),
        compiler_params=pltpu.CompilerParams(
            dimension_semantics=("arbitrary", "arbitrary")),
        name="modulation",
    )(c, ada_w, ada_b.reshape(depth, 1, n))


def _mixer_kernel(x_ref, mod_ref, pre_g_ref, post_g_ref, w_main_ref, w_glu_ref, gate_b_ref,
                  qkc_w_ref, qkc_b_ref, mln_g_ref, cv_w_ref, cv_b_ref, ln_g_ref, ln_b_ref,
                  w_out_ref, out_ref,
                  u_s, proj_buf, q_s, kt_s, cv_buf, y_s, hy_s, cn_s, m_s):
    T = SEQ_TILE
    D = D_MODEL
    L = ML_CHUNK
    t_idx = pl.program_id(1)

    @pl.when(t_idx == 0)
    def _():
        proj_buf[:, 0:QK_HALO, :] = jnp.zeros((proj_buf.shape[0], QK_HALO, LANES), _F32)
        cv_buf[:, 0:CV_HALO, :] = jnp.zeros((cv_buf.shape[0], CV_HALO, LANES), _F32)
        cn_s[...] = jnp.zeros(cn_s.shape, _F32)
        m_s[...] = jnp.zeros(m_s.shape, _F32)

    x = x_ref[0]
    mod = mod_ref[0, 0]
    sh1 = mod[:, 0:D]
    sc1 = mod[:, D:2 * D]
    g1 = mod[:, 2 * D:3 * D]
    u = (x * _rms_scale(x)) * (pre_g_ref[...] * (1.0 + sc1)) + sh1
    u_s[...] = u.astype(_BF16)

    for jb in range(D_CV // GLU_COL_BLOCK):
        c0 = jb * GLU_COL_BLOCK
        a = _dot(u_s[...], w_glu_ref[0, :, c0:c0 + GLU_COL_BLOCK])
        gt = _dot(u_s[...], w_glu_ref[0, :, c0 + D_CV:c0 + D_CV + GLU_COL_BLOCK])
        glu = a * _sigmoid(gt)
        for jj in range(GLU_COL_BLOCK // LANES):
            cv_buf[jb * (GLU_COL_BLOCK // LANES) + jj, CV_HALO:CV_HALO + T, :] = (
                glu[:, jj * LANES:(jj + 1) * LANES])
    cv_off = CV_HALO - (CV_KERNEL - 1)
    for j in range(N_CV_LT):
        for i in range(T // CV_ROWS):
            r0 = i * CV_ROWS + cv_off
            acc = cv_b_ref[j] + cv_w_ref[j, 0:1, :] * cv_buf[j, r0:r0 + CV_ROWS, :]
            for k in range(1, CV_KERNEL):
                acc = acc + cv_w_ref[j, k:k + 1, :] * cv_buf[j, r0 + k:r0 + k + CV_ROWS, :]
            y_s[j, i * CV_ROWS:(i + 1) * CV_ROWS, :] = acc
    cv_buf[:, 0:CV_HALO, :] = cv_buf[:, T:T + CV_HALO, :]

    tot = y_s[0]
    for j in range(1, N_CV_LT):
        tot = tot + y_s[j]
    mu = jnp.broadcast_to(jnp.sum(tot, axis=-1, keepdims=True) * (1.0 / D_CV), (T, LANES))
    sq = jnp.square(y_s[0] - mu)
    for j in range(1, N_CV_LT):
        sq = sq + jnp.square(y_s[j] - mu)
    var = jnp.sum(sq, axis=-1, keepdims=True) * (1.0 / D_CV)
    inv = jnp.broadcast_to(lax.rsqrt(var + EPS), (T, LANES))
    for j in range(N_CV_LT):
        lcols = slice(j * LANES, (j + 1) * LANES)
        yn = (y_s[j] - mu) * inv * ln_g_ref[:, lcols] + ln_b_ref[:, lcols]
        hy_s[:, D_ML + j * LANES:D_ML + (j + 1) * LANES] = _silu(yn).astype(_BF16)

    gates = _dot(u_s[...], w_main_ref[0, :, GATE_COL:GATE_COL + LANES]) + gate_b_ref[...]
    lane_t = lax.broadcasted_iota(jnp.int32, (T, LANES), 1)
    log_f = jnp.minimum(gates, 0.0) - jnp.log1p(jnp.exp(-jnp.abs(gates)))
    z = jnp.where(lane_t < N_HEADS, gates, log_f)
    row_i = lax.broadcasted_iota(jnp.int32, (L, L), 0)
    col_i = lax.broadcasted_iota(jnp.int32, (L, L), 1)
    causal = row_i >= col_i
    tril = causal.astype(_F32)
    lane_l = lax.broadcasted_iota(jnp.int32, (L, LANES), 1)
    zb_chunks, zbt_chunks = [], []
    for c in range(T // L):
        zc = z[c * L:(c + 1) * L, :]
        csum = jnp.dot(tril, zc, precision=lax.Precision.HIGHEST, preferred_element_type=_F32)
        zb = jnp.where(lane_l < N_HEADS, zc, csum)
        zb_chunks.append(zb)
        zbt_chunks.append(zb.T)

    off = QK_HALO - (QK_CONV - 1)

    def qk_conv(slab, col0):
        cols = slice(col0, col0 + LANES)
        acc = qkc_b_ref[:, cols] + qkc_w_ref[0:1, cols] * proj_buf[slab, off:off + T, :]
        for k in range(1, QK_CONV):
            acc = acc + qkc_w_ref[k:k + 1, cols] * proj_buf[slab, off + k:off + k + T, :]
        return _silu(acc)

    ones_blk = jnp.ones((L, LANES), _BF16)
    mix = None
    for h in range(N_HEADS):
        hc = slice(h * HEAD_DIM, (h + 1) * HEAD_DIM)
        slab_q = h * 4 * LT_PER_HEAD
        slab_k = slab_q + LT_PER_HEAD
        slab_v = slab_k + LT_PER_HEAD
        slab_o = slab_v + LT_PER_HEAD
        for g in range(4):
            c0 = g * D_ML + h * HEAD_DIM
            res = _dot(u_s[...], w_main_ref[0, :, c0:c0 + HEAD_DIM])
            for jj in range(LT_PER_HEAD):
                proj_buf[slab_q + g * LT_PER_HEAD + jj, QK_HALO:QK_HALO + T, :] = (
                    res[:, jj * LANES:(jj + 1) * LANES])
        for jj in range(LT_PER_HEAD):
            lt = h * LT_PER_HEAD + jj
            q_s[:, lt * LANES:(lt + 1) * LANES] = qk_conv(slab_q + jj, lt * LANES).astype(_BF16)
            kt_s[lt * LANES:(lt + 1) * LANES, :] = (
                qk_conv(slab_k + jj, D_ML + lt * LANES) * (HEAD_DIM ** -0.5)).T
        proj_buf[slab_q:slab_v, 0:QK_HALO, :] = proj_buf[slab_q:slab_v, T:T + QK_HALO, :]

        cn = cn_s[h]
        m_prev = m_s[h, 0:1, :]
        for c in range(T // L):
            r = slice(c * L, (c + 1) * L)
            pr = slice(QK_HALO + c * L, QK_HALO + (c + 1) * L)
            zb, zbt = zb_chunks[c], zbt_chunks[c]
            b_rep = jnp.broadcast_to(zb[:, N_HEADS + h:N_HEADS + h + 1], (L, LANES))
            i_row = zbt[h:h + 1, :]
            b_row = zbt[N_HEADS + h:N_HEADS + h + 1, :]
            q = q_s[r, hc]
            kt = kt_s[hc, r]
            v_aug = jnp.concatenate(
                [proj_buf[slab_v + jj, pr, :].astype(_BF16)
                 for jj in range(LT_PER_HEAD)] + [ones_blk], axis=1)
            dmat = jnp.where(causal, b_rep + (i_row - b_row), _NEG_INF)
            bm = b_rep + m_prev
            m_t = jnp.maximum(bm, jnp.broadcast_to(jnp.max(dmat, axis=-1, keepdims=True), (L, LANES)))
            s = _dot(q, kt.astype(_BF16)) * jnp.exp(dmat - m_t)
            intra = _dot(s.astype(_BF16), v_aug)
            inter = _dot(q, cn.astype(_BF16))
            w_inter = jnp.exp(bm - m_t)
            comb = [w_inter * inter[:, jj * LANES:(jj + 1) * LANES] + intra[:, jj * LANES:(jj + 1) * LANES]
                    for jj in range(LT_PER_HEAD + 1)]
            den = comb[LT_PER_HEAD]
            inv_den = 1.0 / jnp.maximum(jnp.abs(den), jnp.exp(-m_t))
            hh = [comb[jj] * inv_den for jj in range(LT_PER_HEAD)]
            hsq = hh[0] * hh[0]
            for jj in range(1, LT_PER_HEAD):
                hsq = hsq + hh[jj] * hh[jj]
            rs = jnp.broadcast_to(
                lax.rsqrt(jnp.sum(hsq, axis=-1, keepdims=True) * (1.0 / HEAD_DIM) + EPS), (L, LANES))
            for jj in range(LT_PER_HEAD):
                cols = slice(h * HEAD_DIM + jj * LANES, h * HEAD_DIM + (jj + 1) * LANES)
                o_pre = proj_buf[slab_o + jj, pr, :]
                hy_s[r, cols] = (hh[jj] * rs * mln_g_ref[:, cols] * _sigmoid(o_pre)).astype(_BF16)
            b_last = b_rep[L - 1:L, :]
            g_row = b_last - b_row + i_row
            m_new = jnp.maximum(
                b_last + m_prev,
                jnp.broadcast_to(jnp.max(g_row, axis=-1, keepdims=True), (1, LANES)))
            kwt = kt * jnp.exp(g_row - m_new)
            upd = _dot(kwt.astype(_BF16), v_aug)
            decay = jnp.exp(b_last + m_prev - m_new)
            cn = jnp.concatenate(
                [decay * cn[:, jj * LANES:(jj + 1) * LANES] + upd[:, jj * LANES:(jj + 1) * LANES]
                 for jj in range(LT_PER_HEAD + 1)], axis=1)
            m_prev = m_new
        cn_s[h] = cn
        m_s[h] = jnp.broadcast_to(m_prev, (SUBLANES, LANES))
        part = _dot(hy_s[:, hc], w_out_ref[0, hc, :])
        mix = part if mix is None else mix + part
    mix = mix + _dot(hy_s[:, D_ML:D_ML + D_CV], w_out_ref[0, D_ML:D_ML + D_CV, :])

    out_ref[0] = x + (mix * _rms_scale(mix)) * (g1 * post_g_ref[...])


def _const_spec(shape, index=None):
    index = index if index is not None else (0,) * len(shape)
    return pl.BlockSpec(shape, lambda b, t: index, pipeline_mode=pl.Buffered(1))


def _layer_spec(arr, l):
    return _const_spec((1,) + arr.shape[1:], (l,) + (0,) * (arr.ndim - 1))


def _mixer(l, x, mod_all, pre_g, post_g, w_main, w_glu, qkc_w, qkc_b, igate_b, fgate_b, mln_g,
           cv_w, cv_b, ln_g, ln_b, w_out):
    bsz, seq, d = x.shape
    T = SEQ_TILE
    gate_b = jnp.pad(jnp.concatenate([igate_b, fgate_b]), (0, LANES - N_GATE)).reshape(1, LANES)
    cv_w3 = jnp.pad(cv_w, ((0, 4 * SUBLANES - CV_KERNEL), (0, 0)))
    cv_w3 = cv_w3.reshape(4 * SUBLANES, N_CV_LT, LANES).transpose(1, 0, 2)
    cv_b3 = cv_b.reshape(N_CV_LT, 1, LANES)
    qkc_w8 = jnp.pad(qkc_w, ((0, SUBLANES - QK_CONV), (0, 0)))

    operands = [
        (x, pl.BlockSpec((1, T, d), lambda b, t: (b, t, 0))),
        (mod_all, pl.BlockSpec((1, 1, 1, N_MOD * d), lambda b, t: (l, b, 0, 0))),
        (pre_g.reshape(1, d), None),
        (post_g.reshape(1, d), None),
        (w_main, _layer_spec(w_main, l)),
        (w_glu, _layer_spec(w_glu, l)),
        (gate_b, None),
        (qkc_w8, None), (qkc_b.reshape(1, 2 * D_ML), None), (mln_g.reshape(1, D_ML), None),
        (cv_w3, None), (cv_b3, None), (ln_g.reshape(1, D_CV), None), (ln_b.reshape(1, D_CV), None),
        (w_out, _layer_spec(w_out, l)),
    ]
    args = [a for a, _ in operands]
    in_specs = [s if s is not None else _const_spec(a.shape) for a, s in operands]
    scratch = [
        pltpu.VMEM((T, d), _BF16),
        pltpu.VMEM((4 * D_ML // LANES, QK_HALO + T, LANES), _F32),
        pltpu.VMEM((T, D_ML), _BF16),
        pltpu.VMEM((D_ML, T), _F32),
        pltpu.VMEM((N_CV_LT, CV_HALO + T, LANES), _F32),
        pltpu.VMEM((N_CV_LT, T, LANES), _F32),
        pltpu.VMEM((T, D_ML + D_CV), _BF16),
        pltpu.VMEM((N_HEADS, HEAD_DIM, HEAD_DIM + LANES), _F32),
        pltpu.VMEM((N_HEADS, SUBLANES, LANES), _F32),
    ]
    return pl.pallas_call(
        _mixer_kernel,
        grid=(bsz, seq // T),
        in_specs=in_specs,
        out_specs=pl.BlockSpec((1, T, d), lambda b, t: (b, t, 0)),
        out_shape=jax.ShapeDtypeStruct(x.shape, x.dtype),
        scratch_shapes=scratch,
        compiler_params=pltpu.CompilerParams(
            dimension_semantics=("arbitrary", "arbitrary"),
            vmem_limit_bytes=VMEM_LIMIT_BYTES),
        name="mixer",
    )(*args)


def _ffn_kernel(x_ref, mod_ref, pre_g_ref, post_g_ref, up_ref, cw_ref, cb_ref, down_ref,
                out_ref, u_s, buf_s, halo_s, f_s):
    T = SEQ_TILE
    D = D_MODEL
    W = FFN_COL_BLOCK
    t_idx = pl.program_id(1)

    @pl.when(t_idx == 0)
    def _():
        halo_s[...] = jnp.zeros(halo_s.shape, _F32)

    mod = mod_ref[0, 0]
    sh2 = mod[:, 3 * D:4 * D]
    sc2 = mod[:, 4 * D:5 * D]
    g2 = mod[:, 5 * D:6 * D]
    pre_gain = pre_g_ref[...] * (1.0 + sc2)
    post_gain = g2 * post_g_ref[...]
    off = QK_HALO - (FFN_KERNEL - 1)
    n_lt = W // LANES

    x = x_ref[0]
    u_s[...] = ((x * _rms_scale(x)) * pre_gain + sh2).astype(_BF16)

    def up_conv(slot, part, col0):
        res = _dot(u_s[...], up_ref[0, :, col0:col0 + W])
        out = []
        for jj in range(n_lt):
            lt = col0 // LANES + jj
            cols = slice(lt * LANES, (lt + 1) * LANES)
            buf = buf_s.at[slot, part, jj]
            buf[0:QK_HALO, :] = halo_s[lt]
            buf[QK_HALO:QK_HALO + T, :] = res[:, jj * LANES:(jj + 1) * LANES]
            halo_s[lt] = buf[T:T + QK_HALO, :]
            acc = cb_ref[:, cols] + cw_ref[0:1, cols] * buf[off:off + T, :]
            for k in range(1, FFN_KERNEL):
                acc = acc + cw_ref[k:k + 1, cols] * buf[off + k:off + k + T, :]
            out.append(acc)
        return out

    for j in range(D_FF // W):
        a = up_conv(j % 2, 0, j * W)
        gt = up_conv(j % 2, 1, D_FF + j * W)
        for jj in range(n_lt):
            f_s[:, j * W + jj * LANES:j * W + (jj + 1) * LANES] = (
                _silu(gt[jj]) * a[jj]).astype(_BF16)

    f = _dot(f_s[...], down_ref[0])
    out_ref[0] = x + (f * _rms_scale(f)) * post_gain


def _ffn(l, x, mod_all, pre_g, post_g, up, conv_w, conv_b, down):
    bsz, seq, d = x.shape
    T = SEQ_TILE
    cw8 = jnp.pad(conv_w, ((0, SUBLANES - FFN_KERNEL), (0, 0)))
    operands = [
        (x, pl.BlockSpec((1, T, d), lambda b, t: (b, t, 0))),
        (mod_all, pl.BlockSpec((1, 1, 1, N_MOD * d), lambda b, t: (l, b, 0, 0))),
        (pre_g.reshape(1, d), None),
        (post_g.reshape(1, d), None),
        (up, _layer_spec(up, l)),
        (cw8, None),
        (conv_b.reshape(1, 2 * D_FF), None),
        (down, _layer_spec(down, l)),
    ]
    args = [a for a, _ in operands]
    in_specs = [s if s is not None else _const_spec(a.shape) for a, s in operands]
    scratch = [
        pltpu.VMEM((T, d), _BF16),
        pltpu.VMEM((2, 2, FFN_COL_BLOCK // LANES, QK_HALO + T, LANES), _F32),
        pltpu.VMEM((2 * D_FF // LANES, QK_HALO, LANES), _F32),
        pltpu.VMEM((T, D_FF), _BF16),
    ]
    return pl.pallas_call(
        _ffn_kernel,
        grid=(bsz, seq // T),
        in_specs=in_specs,
        out_specs=pl.BlockSpec((1, T, d), lambda b, t: (b, t, 0)),
        out_shape=jax.ShapeDtypeStruct(x.shape, x.dtype),
        scratch_shapes=scratch,
        compiler_params=pltpu.CompilerParams(
            dimension_semantics=("arbitrary", "arbitrary"),
            vmem_limit_bytes=VMEM_LIMIT_BYTES),
        name="ffn",
    )(*args)


def kernel(x, c, ada_w, ada_b, mix_pre_g, mix_post_g, w_in, qk_conv_w, qk_conv_b, igate_b, fgate_b, ml_norm_g, cv_dw_w, cv_dw_b, cv_ln_g, cv_ln_b, w_out, ffn_pre_g, ffn_post_g, ffn_up, ffn_conv_w, ffn_conv_b, ffn_down):
    bsz, seq, d = x.shape
    depth = ada_w.shape[0]
    assert d == D_MODEL and seq % SEQ_TILE == 0
    mod_all = _modulation(c, ada_w, ada_b).reshape(depth, bsz, 1, N_MOD * d)
    w_main = w_in.astype(_BF16)
    w_glu = w_main[:, :, GATE_COL + N_GATE:]
    w_out16 = w_out.astype(_BF16)
    up16 = ffn_up.astype(_BF16)
    down16 = ffn_down.astype(_BF16)
    for l in range(depth):
        x = _mixer(l, x, mod_all, mix_pre_g[l], mix_post_g[l], w_main, w_glu, qk_conv_w[l],
                   qk_conv_b[l], igate_b[l], fgate_b[l], ml_norm_g[l], cv_dw_w[l], cv_dw_b[l],
                   cv_ln_g[l], cv_ln_b[l], w_out16)
        x = _ffn(l, x, mod_all, ffn_pre_g[l], ffn_post_g[l], up16, ffn_conv_w[l],
                 ffn_conv_b[l], down16)
    return x
```

```python
import jax
import jax.numpy as jnp
from jax import lax
from jax.experimental import pallas as pl
from jax.experimental.pallas import tpu as pltpu

D_MODEL = 1024
N_HEADS = 4
HEAD_DIM = 256
D_ML = N_HEADS * HEAD_DIM
D_CV = 1024
QK_CONV = 4
CV_KERNEL = 31
ML_CHUNK = 128
D_FF = 2816
FFN_KERNEL = 3
N_MOD = 6
N_GATE = 2 * N_HEADS
EPS = 1e-6

LANES = 128
SUBLANES = 8
SEQ_TILE = 512
QK_HALO = SUBLANES
CV_HALO = 4 * SUBLANES
CV_ROWS = 16
FFN_COL_BLOCK = 256
VMEM_LIMIT_BYTES = 58 * 1024 * 1024

LT_PER_HEAD = HEAD_DIM // LANES
N_CV_LT = D_CV // LANES
GATE_COL = 4 * D_ML
GLU_COL_BLOCK = 256

_F32 = jnp.float32
_BF16 = jnp.bfloat16
_NEG_INF = float("-inf")


def _dot(a, b):
    return jnp.dot(a, b, preferred_element_type=_F32)


def _sigmoid(x):
    return 0.5 * jnp.tanh(0.5 * x) + 0.5


def _silu(x):
    h = 0.5 * x
    return h * jnp.tanh(h) + h


def _rms_scale(x):
    return lax.rsqrt(jnp.mean(x * x, axis=-1, keepdims=True) + EPS)


def _mod_kernel(c_ref, w_ref, b_ref, o_ref):
    cond = _silu(c_ref[...]).astype(_BF16)
    o_ref[0] = _dot(cond, w_ref[0].astype(_BF16)) + b_ref[0]


def _modulation(c, ada_w, ada_b):
    depth, d, n = ada_w.shape
    bsz = c.shape[0]
    nb = n // d
    return pl.pallas_call(
        _mod_kernel,
        grid=(depth, nb),
        in_specs=[
            pl.BlockSpec((bsz, d), lambda l, j: (0, 0)),
            pl.BlockSpec((1, d, d), lambda l, j: (l, 0, j)),
            pl.BlockSpec((1, 1, d), lambda l, j: (l, 0, j)),
        ],
        out_specs=pl.BlockSpec((1, bsz, d), lambda l, j: (l, 0, j)),
        out_shape=jax.ShapeDtypeStruct((depth, bsz, n), _F32),
        compiler_params=pltpu.CompilerParams(
            dimension_semantics=("arbitrary", "arbitrary")),
        name="modulation",
    )(c, ada_w, ada_b.reshape(depth, 1, n))


def _mixer_kernel(x_ref, mod_ref, pre_g_ref, post_g_ref, w_main_ref, w_glu_ref, gate_b_ref,
                  qkc_w_ref, qkc_b_ref, mln_g_ref, cv_w_ref, cv_b_ref, ln_g_ref, ln_b_ref,
                  w_out_ref, out_ref,
                  u_s, proj_buf, q_s, kt_s, cv_buf, y_s, hy_s, cn_s, m_s):
    T = SEQ_TILE
    D = D_MODEL
    L = ML_CHUNK
    t_idx = pl.program_id(1)

    @pl.when(t_idx == 0)
    def _():
        proj_buf[:, 0:QK_HALO, :] = jnp.zeros((proj_buf.shape[0], QK_HALO, LANES), _F32)
        cv_buf[:, 0:CV_HALO, :] = jnp.zeros((cv_buf.shape[0], CV_HALO, LANES), _F32)
        cn_s[...] = jnp.zeros(cn_s.shape, _F32)
        m_s[...] = jnp.zeros(m_s.shape, _F32)

    x = x_ref[0]
    mod = mod_ref[0, 0]
    sh1 = mod[:, 0:D]
    sc1 = mod[:, D:2 * D]
    g1 = mod[:, 2 * D:3 * D]
    u = (x * _rms_scale(x)) * (pre_g_ref[...] * (1.0 + sc1)) + sh1
    u_s[...] = u.astype(_BF16)

    for jb in range(D_CV // GLU_COL_BLOCK):
        c0 = jb * GLU_COL_BLOCK
        a = _dot(u_s[...], w_glu_ref[0, :, c0:c0 + GLU_COL_BLOCK])
        gt = _dot(u_s[...], w_glu_ref[0, :, c0 + D_CV:c0 + D_CV + GLU_COL_BLOCK])
        glu = a * _sigmoid(gt)
        for jj in range(GLU_COL_BLOCK // LANES):
            cv_buf[jb * (GLU_COL_BLOCK // LANES) + jj, CV_HALO:CV_HALO + T, :] = (
                glu[:, jj * LANES:(jj + 1) * LANES])
    cv_off = CV_HALO - (CV_KERNEL - 1)
    for j in range(N_CV_LT):
        for i in range(T // CV_ROWS):
            r0 = i * CV_ROWS + cv_off
            acc = cv_b_ref[j] + cv_w_ref[j, 0:1, :] * cv_buf[j, r0:r0 + CV_ROWS, :]
            for k in range(1, CV_KERNEL):
                acc = acc + cv_w_ref[j, k:k + 1, :] * cv_buf[j, r0 + k:r0 + k + CV_ROWS, :]
            y_s[j, i * CV_ROWS:(i + 1) * CV_ROWS, :] = acc
    cv_buf[:, 0:CV_HALO, :] = cv_buf[:, T:T + CV_HALO, :]

    tot = y_s[0]
    for j in range(1, N_CV_LT):
        tot = tot + y_s[j]
    mu = jnp.broadcast_to(jnp.sum(tot, axis=-1, keepdims=True) * (1.0 / D_CV), (T, LANES))
    sq = jnp.square(y_s[0] - mu)
    for j in range(1, N_CV_LT):
        sq = sq + jnp.square(y_s[j] - mu)
    var = jnp.sum(sq, axis=-1, keepdims=True) * (1.0 / D_CV)
    inv = jnp.broadcast_to(lax.rsqrt(var + EPS), (T, LANES))
    for j in range(N_CV_LT):
        lcols = slice(j * LANES, (j + 1) * LANES)
        yn = (y_s[j] - mu) * inv * ln_g_ref[:, lcols] + ln_b_ref[:, lcols]
        hy_s[:, D_ML + j * LANES:D_ML + (j + 1) * LANES] = _silu(yn).astype(_BF16)

    gates = _dot(u_s[...], w_main_ref[0, :, GATE_COL:GATE_COL + LANES]) + gate_b_ref[...]
    lane_t = lax.broadcasted_iota(jnp.int32, (T, LANES), 1)
    log_f = jnp.minimum(gates, 0.0) - jnp.log1p(jnp.exp(-jnp.abs(gates)))
    z = jnp.where(lane_t < N_HEADS, gates, log_f)
    row_i = lax.broadcasted_iota(jnp.int32, (L, L), 0)
    col_i = lax.broadcasted_iota(jnp.int32, (L, L), 1)
    causal = row_i >= col_i
    tril = causal.astype(_F32)
    lane_l = lax.broadcasted_iota(jnp.int32, (L, LANES), 1)
    zb_chunks, zbt_chunks = [], []
    for c in range(T // L):
        zc = z[c * L:(c + 1) * L, :]
        csum = jnp.dot(tril, zc, precision=lax.Precision.HIGHEST, preferred_element_type=_F32)
        zb = jnp.where(lane_l < N_HEADS, zc, csum)
        zb_chunks.append(zb)
        zbt_chunks.append(zb.T)

    off = QK_HALO - (QK_CONV - 1)

    def qk_conv(slab, col0):
        cols = slice(col0, col0 + LANES)
        acc = qkc_b_ref[:, cols] + qkc_w_ref[0:1, cols] * proj_buf[slab, off:off + T, :]
        for k in range(1, QK_CONV):
            acc = acc + qkc_w_ref[k:k + 1, cols] * proj_buf[slab, off + k:off + k + T, :]
        return _silu(acc)

    ones_blk = jnp.ones((L, LANES), _BF16)
    mix = None
    for h in range(N_HEADS):
        hc = slice(h * HEAD_DIM, (h + 1) * HEAD_DIM)
        slab_q = h * 4 * LT_PER_HEAD
        slab_k = slab_q + LT_PER_HEAD
        slab_v = slab_k + LT_PER_HEAD
        slab_o = slab_v + LT_PER_HEAD
        for g in range(4):
            c0 = g * D_ML + h * HEAD_DIM
            res = _dot(u_s[...], w_main_ref[0, :, c0:c0 + HEAD_DIM])
            for jj in range(LT_PER_HEAD):
                proj_buf[slab_q + g * LT_PER_HEAD + jj, QK_HALO:QK_HALO + T, :] = (
                    res[:, jj * LANES:(jj + 1) * LANES])
        for jj in range(LT_PER_HEAD):
            lt = h * LT_PER_HEAD + jj
            q_s[:, lt * LANES:(lt + 1) * LANES] = qk_conv(slab_q + jj, lt * LANES).astype(_BF16)
            kt_s[lt * LANES:(lt + 1) * LANES, :] = (
                qk_conv(slab_k + jj, D_ML + lt * LANES) * (HEAD_DIM ** -0.5)).T
        proj_buf[slab_q:slab_v, 0:QK_HALO, :] = proj_buf[slab_q:slab_v, T:T + QK_HALO, :]

        cn = cn_s[h]
        m_prev = m_s[h, 0:1, :]
        for c in range(T // L):
            r = slice(c * L, (c + 1) * L)
            pr = slice(QK_HALO + c * L, QK_HALO + (c + 1) * L)
            zb, zbt = zb_chunks[c], zbt_chunks[c]
            b_rep = jnp.broadcast_to(zb[:, N_HEADS + h:N_HEADS + h + 1], (L, LANES))
            i_row = zbt[h:h + 1, :]
            b_row = zbt[N_HEADS + h:N_HEADS + h + 1, :]
            q = q_s[r, hc]
            kt = kt_s[hc, r]
            v_aug = jnp.concatenate(
                [proj_buf[slab_v + jj, pr, :].astype(_BF16)
                 for jj in range(LT_PER_HEAD)] + [ones_blk], axis=1)
            dmat = jnp.where(causal, b_rep + (i_row - b_row), _NEG_INF)
            bm = b_rep + m_prev
            m_t = jnp.maximum(bm, jnp.broadcast_to(jnp.max(dmat, axis=-1, keepdims=True), (L, LANES)))
            s = _dot(q, kt.astype(_BF16)) * jnp.exp(dmat - m_t)
            intra = _dot(s.astype(_BF16), v_aug)
            inter = _dot(q, cn.astype(_BF16))
            w_inter = jnp.exp(bm - m_t)
            comb = [w_inter * inter[:, jj * LANES:(jj + 1) * LANES] + intra[:, jj * LANES:(jj + 1) * LANES]
                    for jj in range(LT_PER_HEAD + 1)]
            den = comb[LT_PER_HEAD]
            inv_den = 1.0 / jnp.maximum(jnp.abs(den), jnp.exp(-m_t))
            hh = [comb[jj] * inv_den for jj in range(LT_PER_HEAD)]
            hsq = hh[0] * hh[0]
            for jj in range(1, LT_PER_HEAD):
                hsq = hsq + hh[jj] * hh[jj]
            rs = jnp.broadcast_to(
                lax.rsqrt(jnp.sum(hsq, axis=-1, keepdims=True) * (1.0 / HEAD_DIM) + EPS), (L, LANES))
            for jj in range(LT_PER_HEAD):
                cols = slice(h * HEAD_DIM + jj * LANES, h * HEAD_DIM + (jj + 1) * LANES)
                o_pre = proj_buf[slab_o + jj, pr, :]
                hy_s[r, cols] = (hh[jj] * rs * mln_g_ref[:, cols] * _sigmoid(o_pre)).astype(_BF16)
            b_last = b_rep[L - 1:L, :]
            g_row = b_last - b_row + i_row
            m_new = jnp.maximum(
                b_last + m_prev,
                jnp.broadcast_to(jnp.max(g_row, axis=-1, keepdims=True), (1, LANES)))
            kwt = kt * jnp.exp(g_row - m_new)
            upd = _dot(kwt.astype(_BF16), v_aug)
            decay = jnp.exp(b_last + m_prev - m_new)
            cn = jnp.concatenate(
                [decay * cn[:, jj * LANES:(jj + 1) * LANES] + upd[:, jj * LANES:(jj + 1) * LANES]
                 for jj in range(LT_PER_HEAD + 1)], axis=1)
            m_prev = m_new
        cn_s[h] = cn
        m_s[h] = jnp.broadcast_to(m_prev, (SUBLANES, LANES))
        part = _dot(hy_s[:, hc], w_out_ref[0, hc, :])
        mix = part if mix is None else mix + part
    mix = mix + _dot(hy_s[:, D_ML:D_ML + D_CV], w_out_ref[0, D_ML:D_ML + D_CV, :])

    out_ref[0] = x + (mix * _rms_scale(mix)) * (g1 * post_g_ref[...])


def _const_spec(shape, index=None):
    index = index if index is not None else (0,) * len(shape)
    return pl.BlockSpec(shape, lambda b, t: index, pipeline_mode=pl.Buffered(1))


def _layer_spec(arr, l):
    return _const_spec((1,) + arr.shape[1:], (l,) + (0,) * (arr.ndim - 1))


def _mixer(l, x, mod_all, pre_g, post_g, w_main, w_glu, qkc_w, qkc_b, igate_b, fgate_b, mln_g,
           cv_w, cv_b, ln_g, ln_b, w_out):
    bsz, seq, d = x.shape
    T = SEQ_TILE
    gate_b = jnp.pad(jnp.concatenate([igate_b, fgate_b]), (0, LANES - N_GATE)).reshape(1, LANES)
    cv_w3 = jnp.pad(cv_w, ((0, 4 * SUBLANES - CV_KERNEL), (0, 0)))
    cv_w3 = cv_w3.reshape(4 * SUBLANES, N_CV_LT, LANES).transpose(1, 0, 2)
    cv_b3 = cv_b.reshape(N_CV_LT, 1, LANES)
    qkc_w8 = jnp.pad(qkc_w, ((0, SUBLANES - QK_CONV), (0, 0)))

    operands = [
        (x, pl.BlockSpec((1, T, d), lambda b, t: (b, t, 0))),
        (mod_all, pl.BlockSpec((1, 1, 1, N_MOD * d), lambda b, t: (l, b, 0, 0))),
        (pre_g.reshape(1, d), None),
        (post_g.reshape(1, d), None),
        (w_main, _layer_spec(w_main, l)),
        (w_glu, _layer_spec(w_glu, l)),
        (gate_b, None),
        (qkc_w8, None), (qkc_b.reshape(1, 2 * D_ML), None), (mln_g.reshape(1, D_ML), None),
        (cv_w3, None), (cv_b3, None), (ln_g.reshape(1, D_CV), None), (ln_b.reshape(1, D_CV), None),
        (w_out, _layer_spec(w_out, l)),
    ]
    args = [a for a, _ in operands]
    in_specs = [s if s is not None else _const_spec(a.shape) for a, s in operands]
    scratch = [
        pltpu.VMEM((T, d), _BF16),
        pltpu.VMEM((4 * D_ML // LANES, QK_HALO + T, LANES), _F32),
        pltpu.VMEM((T, D_ML), _BF16),
        pltpu.VMEM((D_ML, T), _F32),
        pltpu.VMEM((N_CV_LT, CV_HALO + T, LANES), _F32),
        pltpu.VMEM((N_CV_LT, T, LANES), _F32),
        pltpu.VMEM((T, D_ML + D_CV), _BF16),
        pltpu.VMEM((N_HEADS, HEAD_DIM, HEAD_DIM + LANES), _F32),
        pltpu.VMEM((N_HEADS, SUBLANES, LANES), _F32),
    ]
    return pl.pallas_call(
        _mixer_kernel,
        grid=(bsz, seq // T),
        in_specs=in_specs,
        out_specs=pl.BlockSpec((1, T, d), lambda b, t: (b, t, 0)),
        out_shape=jax.ShapeDtypeStruct(x.shape, x.dtype),
        scratch_shapes=scratch,
        compiler_params=pltpu.CompilerParams(
            dimension_semantics=("arbitrary", "arbitrary"),
            vmem_limit_bytes=VMEM_LIMIT_BYTES),
        name="mixer",
    )(*args)


def _ffn_kernel(x_ref, mod_ref, pre_g_ref, post_g_ref, up_ref, cw_ref, cb_ref, down_ref,
                out_ref, u_s, buf_s, halo_s, f_s):
    T = SEQ_TILE
    D = D_MODEL
    W = FFN_COL_BLOCK
    t_idx = pl.program_id(1)

    @pl.when(t_idx == 0)
    def _():
        halo_s[...] = jnp.zeros(halo_s.shape, _F32)

    mod = mod_ref[0, 0]
    sh2 = mod[:, 3 * D:4 * D]
    sc2 = mod[:, 4 * D:5 * D]
    g2 = mod[:, 5 * D:6 * D]
    pre_gain = pre_g_ref[...] * (1.0 + sc2)
    post_gain = g2 * post_g_ref[...]
    off = QK_HALO - (FFN_KERNEL - 1)
    n_lt = W // LANES

    x = x_ref[0]
    u_s[...] = ((x * _rms_scale(x)) * pre_gain + sh2).astype(_BF16)

    def up_conv(slot, part, col0):
        res = _dot(u_s[...], up_ref[0, :, col0:col0 + W])
        out = []
        for jj in range(n_lt):
            lt = col0 // LANES + jj
            cols = slice(lt * LANES, (lt + 1) * LANES)
            buf = buf_s.at[slot, part, jj]
            buf[0:QK_HALO, :] = halo_s[lt]
            buf[QK_HALO:QK_HALO + T, :] = res[:, jj * LANES:(jj + 1) * LANES]
            halo_s[lt] = buf[T:T + QK_HALO, :]
            acc = cb_ref[:, cols] + cw_ref[0:1, cols] * buf[off:off + T, :]
            for k in range(1, FFN_KERNEL):
                acc = acc + cw_ref[k:k + 1, cols] * buf[off + k:off + k + T, :]
            out.append(acc)
        return out

    for j in range(D_FF // W):
        a = up_conv(j % 2, 0, j * W)
        gt = up_conv(j % 2, 1, D_FF + j * W)
        for jj in range(n_lt):
            f_s[:, j * W + jj * LANES:j * W + (jj + 1) * LANES] = (
                _silu(gt[jj]) * a[jj]).astype(_BF16)

    f = _dot(f_s[...], down_ref[0])
    out_ref[0] = x + (f * _rms_scale(f)) * post_gain


def _ffn(l, x, mod_all, pre_g, post_g, up, conv_w, conv_b, down):
    bsz, seq, d = x.shape
    T = SEQ_TILE
    cw8 = jnp.pad(conv_w, ((0, SUBLANES - FFN_KERNEL), (0, 0)))
    operands = [
        (x, pl.BlockSpec((1, T, d), lambda b, t: (b, t, 0))),
        (mod_all, pl.BlockSpec((1, 1, 1, N_MOD * d), lambda b, t: (l, b, 0, 0))),
        (pre_g.reshape(1, d), None),
        (post_g.reshape(1, d), None),
        (up, _layer_spec(up, l)),
        (cw8, None),
        (conv_b.reshape(1, 2 * D_FF), None),
        (down, _layer_spec(down, l)),
    ]
    args = [a for a, _ in operands]
    in_specs = [s if s is not None else _const_spec(a.shape) for a, s in operands]
    scratch = [
        pltpu.VMEM((T, d), _BF16),
        pltpu.VMEM((2, 2, FFN_COL_BLOCK // LANES, QK_HALO + T, LANES), _F32),
        pltpu.VMEM((2 * D_FF // LANES, QK_HALO, LANES), _F32),
        pltpu.VMEM((T, D_FF), _BF16),
    ]
    return pl.pallas_call(
        _ffn_kernel,
        grid=(bsz, seq // T),
        in_specs=in_specs,
        out_specs=pl.BlockSpec((1, T, d), lambda b, t: (b, t, 0)),
        out_shape=jax.ShapeDtypeStruct(x.shape, x.dtype),
        scratch_shapes=scratch,
        compiler_params=pltpu.CompilerParams(
            dimension_semantics=("arbitrary", "arbitrary"),
            vmem_limit_bytes=VMEM_LIMIT_BYTES),
        name="ffn",
    )(*args)


def kernel(x, c, ada_w, ada_b, mix_pre_g, mix_post_g, w_in, qk_conv_w, qk_conv_b, igate_b, fgate_b, ml_norm_g, cv_dw_w, cv_dw_b, cv_ln_g, cv_ln_b, w_out, ffn_pre_g, ffn_post_g, ffn_up, ffn_conv_w, ffn_conv_b, ffn_down):
    bsz, seq, d = x.shape
    depth = ada_w.shape[0]
    assert d == D_MODEL and seq % SEQ_TILE == 0
    mod_all = _modulation(c, ada_w, ada_b).reshape(depth, bsz, 1, N_MOD * d)
    w_main = w_in.astype(_BF16)
    w_glu = w_main[:, :, GATE_COL + N_GATE:]
    w_out16 = w_out.astype(_BF16)
    up16 = ffn_up.astype(_BF16)
    down16 = ffn_down.astype(_BF16)
    for l in range(depth):
        x = _mixer(l, x, mod_all, mix_pre_g[l], mix_post_g[l], w_main, w_glu, qk_conv_w[l],
                   qk_conv_b[l], igate_b[l], fgate_b[l], ml_norm_g[l], cv_dw_w[l], cv_dw_b[l],
                   cv_ln_g[l], cv_ln_b[l], w_out16)
        x = _ffn(l, x, mod_all, ffn_pre_g[l], ffn_post_g[l], up16, ffn_conv_w[l],
                 ffn_conv_b[l], down16)
    return x
```

```python
import jax
import jax.numpy as jnp
from jax import lax
from jax.experimental import pallas as pl
from jax.experimental.pallas import tpu as pltpu

D_MODEL = 1024
N_HEADS = 4
HEAD_DIM = 256
D_ML = N_HEADS * HEAD_DIM
D_CV = 1024
QK_CONV = 4
CV_KERNEL = 31
ML_CHUNK = 128
D_FF = 2816
FFN_KERNEL = 3
N_MOD = 6
N_GATE = 2 * N_HEADS
EPS = 1e-6

LANES = 128
SUBLANES = 8
SEQ_TILE = 512
QK_HALO = SUBLANES
CV_HALO = 4 * SUBLANES
CV_ROWS = 16
FFN_COL_BLOCK = 256
VMEM_LIMIT_BYTES = 58 * 1024 * 1024

LT_PER_HEAD = HEAD_DIM // LANES
N_CV_LT = D_CV // LANES
GATE_COL = 4 * D_ML
GLU_COL_BLOCK = 512

_F32 = jnp.float32
_BF16 = jnp.bfloat16
_NEG_INF = float("-inf")


def _dot(a, b):
    return jnp.dot(a, b, preferred_element_type=_F32)


def _sigmoid(x):
    return 0.5 * jnp.tanh(0.5 * x) + 0.5


def _silu(x):
    h = 0.5 * x
    return h * jnp.tanh(h) + h


def _rms_scale(x):
    return lax.rsqrt(jnp.mean(x * x, axis=-1, keepdims=True) + EPS)


def _mod_kernel(c_ref, w_ref, b_ref, o_ref):
    cond = _silu(c_ref[...]).astype(_BF16)
    o_ref[0] = _dot(cond, w_ref[0].astype(_BF16)) + b_ref[0]


def _modulation(c, ada_w, ada_b):
    depth, d, n = ada_w.shape
    bsz = c.shape[0]
    nb = n // d
    return pl.pallas_call(
        _mod_kernel,
        grid=(depth, nb),
        in_specs=[
            pl.BlockSpec((bsz, d), lambda l, j: (0, 0)),
            pl.BlockSpec((1, d, d), lambda l, j: (l, 0, j)),
            pl.BlockSpec((1, 1, d), lambda l, j: (l, 0, j)),
        ],
        out_specs=pl.BlockSpec((1, bsz, d), lambda l, j: (l, 0, j)),
        out_shape=jax.ShapeDtypeStruct((depth, bsz, n), _F32),
        compiler_params=pltpu.CompilerParams(
            dimension_semantics=("arbitrary", "arbitrary")),
        name="modulation",
    )(c, ada_w, ada_b.reshape(depth, 1, n))


def _mixer_kernel(x_ref, mod_ref, pre_g_ref, post_g_ref, w_main_ref, w_glu_ref, gate_b_ref,
                  qkc_w_ref, qkc_b_ref, mln_g_ref, cv_w_ref, cv_b_ref, ln_g_ref, ln_b_ref,
                  w_out_ref, out_ref,
                  u_s, proj_buf, q_s, kt_s, cv_buf, y_s, hy_s, cn_s, cn16_s, m_s):
    T = SEQ_TILE
    D = D_MODEL
    L = ML_CHUNK
    t_idx = pl.program_id(1)

    @pl.when(t_idx == 0)
    def _():
        proj_buf[:, 0:QK_HALO, :] = jnp.zeros((proj_buf.shape[0], QK_HALO, LANES), _F32)
        cv_buf[:, 0:CV_HALO, :] = jnp.zeros((cv_buf.shape[0], CV_HALO, LANES), _F32)
        cn_s[...] = jnp.zeros(cn_s.shape, _F32)
        cn16_s[...] = jnp.zeros(cn16_s.shape, _BF16)
        m_s[...] = jnp.zeros(m_s.shape, _F32)

    x = x_ref[0]
    mod = mod_ref[0, 0]
    sh1 = mod[:, 0:D]
    sc1 = mod[:, D:2 * D]
    g1 = mod[:, 2 * D:3 * D]
    u = (x * _rms_scale(x)) * (pre_g_ref[...] * (1.0 + sc1)) + sh1
    u_s[...] = u.astype(_BF16)

    for jb in range(D_CV // GLU_COL_BLOCK):
        c0 = jb * GLU_COL_BLOCK
        a = _dot(u_s[...], w_glu_ref[0, :, c0:c0 + GLU_COL_BLOCK])
        gt = _dot(u_s[...], w_glu_ref[0, :, c0 + D_CV:c0 + D_CV + GLU_COL_BLOCK])
        glu = a * _sigmoid(gt)
        for jj in range(GLU_COL_BLOCK // LANES):
            cv_buf[jb * (GLU_COL_BLOCK // LANES) + jj, CV_HALO:CV_HALO + T, :] = (
                glu[:, jj * LANES:(jj + 1) * LANES])
    cv_off = CV_HALO - (CV_KERNEL - 1)
    for j in range(N_CV_LT):
        for i in range(T // CV_ROWS):
            r0 = i * CV_ROWS + cv_off
            acc = cv_b_ref[j] + cv_w_ref[j, 0:1, :] * cv_buf[j, r0:r0 + CV_ROWS, :]
            for k in range(1, CV_KERNEL):
                acc = acc + cv_w_ref[j, k:k + 1, :] * cv_buf[j, r0 + k:r0 + k + CV_ROWS, :]
            y_s[j, i * CV_ROWS:(i + 1) * CV_ROWS, :] = acc
    cv_buf[:, 0:CV_HALO, :] = cv_buf[:, T:T + CV_HALO, :]

    tot = y_s[0]
    for j in range(1, N_CV_LT):
        tot = tot + y_s[j]
    mu = jnp.broadcast_to(jnp.sum(tot, axis=-1, keepdims=True) * (1.0 / D_CV), (T, LANES))
    sq = jnp.square(y_s[0] - mu)
    for j in range(1, N_CV_LT):
        sq = sq + jnp.square(y_s[j] - mu)
    var = jnp.sum(sq, axis=-1, keepdims=True) * (1.0 / D_CV)
    inv = jnp.broadcast_to(lax.rsqrt(var + EPS), (T, LANES))
    for j in range(N_CV_LT):
        lcols = slice(j * LANES, (j + 1) * LANES)
        yn = (y_s[j] - mu) * inv * ln_g_ref[:, lcols] + ln_b_ref[:, lcols]
        hy_s[:, D_ML + j * LANES:D_ML + (j + 1) * LANES] = _silu(yn).astype(_BF16)

    gates = _dot(u_s[...], w_main_ref[0, :, GATE_COL:GATE_COL + LANES]) + gate_b_ref[...]
    lane_t = lax.broadcasted_iota(jnp.int32, (T, LANES), 1)
    log_f = jnp.minimum(gates, 0.0) - jnp.log1p(jnp.exp(-jnp.abs(gates)))
    z = jnp.where(lane_t < N_HEADS, gates, log_f)
    row_i = lax.broadcasted_iota(jnp.int32, (L, L), 0)
    col_i = lax.broadcasted_iota(jnp.int32, (L, L), 1)
    causal = row_i >= col_i
    tril = causal.astype(_F32)
    lane_l = lax.broadcasted_iota(jnp.int32, (L, LANES), 1)
    zb_chunks, zbt_chunks = [], []
    for c in range(T // L):
        zc = z[c * L:(c + 1) * L, :]
        csum = jnp.dot(tril, zc, precision=lax.Precision.HIGHEST, preferred_element_type=_F32)
        zb = jnp.where(lane_l < N_HEADS, zc, csum)
        zb_chunks.append(zb)
        zbt_chunks.append(zb.T)

    off = QK_HALO - (QK_CONV - 1)

    def qk_conv(slab, col0):
        cols = slice(col0, col0 + LANES)
        acc = qkc_b_ref[:, cols] + qkc_w_ref[0:1, cols] * proj_buf[slab, off:off + T, :]
        for k in range(1, QK_CONV):
            acc = acc + qkc_w_ref[k:k + 1, cols] * proj_buf[slab, off + k:off + k + T, :]
        return _silu(acc)

    ones_blk = jnp.ones((L, LANES), _BF16)
    mix = None
    for h in range(N_HEADS):
        hc = slice(h * HEAD_DIM, (h + 1) * HEAD_DIM)
        slab_q = h * 4 * LT_PER_HEAD
        slab_k = slab_q + LT_PER_HEAD
        slab_v = slab_k + LT_PER_HEAD
        slab_o = slab_v + LT_PER_HEAD
        for g in range(4):
            c0 = g * D_ML + h * HEAD_DIM
            res = _dot(u_s[...], w_main_ref[0, :, c0:c0 + HEAD_DIM])
            for jj in range(LT_PER_HEAD):
                proj_buf[slab_q + g * LT_PER_HEAD + jj, QK_HALO:QK_HALO + T, :] = (
                    res[:, jj * LANES:(jj + 1) * LANES])
        for jj in range(LT_PER_HEAD):
            lt = h * LT_PER_HEAD + jj
            q_s[:, lt * LANES:(lt + 1) * LANES] = qk_conv(slab_q + jj, lt * LANES).astype(_BF16)
            kt_s[lt * LANES:(lt + 1) * LANES, :] = (
                qk_conv(slab_k + jj, D_ML + lt * LANES) * (HEAD_DIM ** -0.5)).T
        proj_buf[slab_q:slab_v, 0:QK_HALO, :] = proj_buf[slab_q:slab_v, T:T + QK_HALO, :]

        m_prev = m_s[h, 0:1, :]
        for c in range(T // L):
            r = slice(c * L, (c + 1) * L)
            pr = slice(QK_HALO + c * L, QK_HALO + (c + 1) * L)
            zb, zbt = zb_chunks[c], zbt_chunks[c]
            b_rep = jnp.broadcast_to(zb[:, N_HEADS + h:N_HEADS + h + 1], (L, LANES))
            i_row = zbt[h:h + 1, :]
            b_row = zbt[N_HEADS + h:N_HEADS + h + 1, :]
            q = q_s[r, hc]
            kt = kt_s[hc, r]
            v_aug = jnp.concatenate(
                [proj_buf[slab_v + jj, pr, :].astype(_BF16)
                 for jj in range(LT_PER_HEAD)] + [ones_blk], axis=1)
            dmat = jnp.where(causal, b_rep + (i_row - b_row), _NEG_INF)
            bm = b_rep + m_prev
            m_t = jnp.maximum(bm, jnp.broadcast_to(jnp.max(dmat, axis=-1, keepdims=True), (L, LANES)))
            s = _dot(q, kt.astype(_BF16)) * jnp.exp(dmat - m_t)
            intra = _dot(s.astype(_BF16), v_aug)
            inter = _dot(q, cn16_s[h])
            w_inter = jnp.exp(bm - m_t)
            comb = [w_inter * inter[:, jj * LANES:(jj + 1) * LANES] + intra[:, jj * LANES:(jj + 1) * LANES]
                    for jj in range(LT_PER_HEAD + 1)]
            den = comb[LT_PER_HEAD]
            inv_den = 1.0 / jnp.maximum(jnp.abs(den), jnp.exp(-m_t))
            hh = [comb[jj] * inv_den for jj in range(LT_PER_HEAD)]
            hsq = hh[0] * hh[0]
            for jj in range(1, LT_PER_HEAD):
                hsq = hsq + hh[jj] * hh[jj]
            rs = jnp.broadcast_to(
                lax.rsqrt(jnp.sum(hsq, axis=-1, keepdims=True) * (1.0 / HEAD_DIM) + EPS), (L, LANES))
            for jj in range(LT_PER_HEAD):
                cols = slice(h * HEAD_DIM + jj * LANES, h * HEAD_DIM + (jj + 1) * LANES)
                o_pre = proj_buf[slab_o + jj, pr, :]
                hy_s[r, cols] = (hh[jj] * rs * mln_g_ref[:, cols] * _sigmoid(o_pre)).astype(_BF16)
            b_last = b_rep[L - 1:L, :]
            g_row = b_last - b_row + i_row
            m_new = jnp.maximum(
                b_last + m_prev,
                jnp.broadcast_to(jnp.max(g_row, axis=-1, keepdims=True), (1, LANES)))
            kwt = kt * jnp.exp(g_row - m_new)
            upd = _dot(kwt.astype(_BF16), v_aug)
            decay = jnp.exp(b_last + m_prev - m_new)
            for jj in range(LT_PER_HEAD + 1):
                lt_cols = slice(jj * LANES, (jj + 1) * LANES)
                new = decay * cn_s[h, :, lt_cols] + upd[:, lt_cols]
                cn_s[h, :, lt_cols] = new
                cn16_s[h, :, lt_cols] = new.astype(_BF16)
            m_prev = m_new
        m_s[h] = jnp.broadcast_to(m_prev, (SUBLANES, LANES))
        part = _dot(hy_s[:, hc], w_out_ref[0, hc, :])
        mix = part if mix is None else mix + part
    mix = mix + _dot(hy_s[:, D_ML:D_ML + D_CV], w_out_ref[0, D_ML:D_ML + D_CV, :])

    out_ref[0] = x + (mix * _rms_scale(mix)) * (g1 * post_g_ref[...])


def _const_spec(shape, index=None):
    index = index if index is not None else (0,) * len(shape)
    return pl.BlockSpec(shape, lambda b, t: index, pipeline_mode=pl.Buffered(1))


def _layer_spec(arr, l):
    return _const_spec((1,) + arr.shape[1:], (l,) + (0,) * (arr.ndim - 1))


def _mixer(l, x, mod_all, pre_g, post_g, w_main, w_glu, qkc_w, qkc_b, igate_b, fgate_b, mln_g,
           cv_w, cv_b, ln_g, ln_b, w_out):
    bsz, seq, d = x.shape
    T = SEQ_TILE
    gate_b = jnp.pad(jnp.concatenate([igate_b, fgate_b]), (0, LANES - N_GATE)).reshape(1, LANES)
    cv_w3 = jnp.pad(cv_w, ((0, 4 * SUBLANES - CV_KERNEL), (0, 0)))
    cv_w3 = cv_w3.reshape(4 * SUBLANES, N_CV_LT, LANES).transpose(1, 0, 2)
    cv_b3 = cv_b.reshape(N_CV_LT, 1, LANES)
    qkc_w8 = jnp.pad(qkc_w, ((0, SUBLANES - QK_CONV), (0, 0)))

    operands = [
        (x, pl.BlockSpec((1, T, d), lambda b, t: (b, t, 0))),
        (mod_all, pl.BlockSpec((1, 1, 1, N_MOD * d), lambda b, t: (l, b, 0, 0))),
        (pre_g.reshape(1, d), None),
        (post_g.reshape(1, d), None),
        (w_main, _layer_spec(w_main, l)),
        (w_glu, _layer_spec(w_glu, l)),
        (gate_b, None),
        (qkc_w8, None), (qkc_b.reshape(1, 2 * D_ML), None), (mln_g.reshape(1, D_ML), None),
        (cv_w3, None), (cv_b3, None), (ln_g.reshape(1, D_CV), None), (ln_b.reshape(1, D_CV), None),
        (w_out, _layer_spec(w_out, l)),
    ]
    args = [a for a, _ in operands]
    in_specs = [s if s is not None else _const_spec(a.shape) for a, s in operands]
    scratch = [
        pltpu.VMEM((T, d), _BF16),
        pltpu.VMEM((4 * D_ML // LANES, QK_HALO + T, LANES), _F32),
        pltpu.VMEM((T, D_ML), _BF16),
        pltpu.VMEM((D_ML, T), _F32),
        pltpu.VMEM((N_CV_LT, CV_HALO + T, LANES), _F32),
        pltpu.VMEM((N_CV_LT, T, LANES), _F32),
        pltpu.VMEM((T, D_ML + D_CV), _BF16),
        pltpu.VMEM((N_HEADS, HEAD_DIM, HEAD_DIM + LANES), _F32),
        pltpu.VMEM((N_HEADS, HEAD_DIM, HEAD_DIM + LANES), _BF16),
        pltpu.VMEM((N_HEADS, SUBLANES, LANES), _F32),
    ]
    return pl.pallas_call(
        _mixer_kernel,
        grid=(bsz, seq // T),
        in_specs=in_specs,
        out_specs=pl.BlockSpec((1, T, d), lambda b, t: (b, t, 0)),
        out_shape=jax.ShapeDtypeStruct(x.shape, x.dtype),
        scratch_shapes=scratch,
        compiler_params=pltpu.CompilerParams(
            dimension_semantics=("arbitrary", "arbitrary"),
            vmem_limit_bytes=VMEM_LIMIT_BYTES),
        name="mixer",
    )(*args)


def _ffn_kernel(x_ref, mod_ref, pre_g_ref, post_g_ref, up_ref, cw_ref, cb_ref, down_ref,
                out_ref, u_s, buf_s, halo_s, f_s):
    T = SEQ_TILE
    D = D_MODEL
    W = FFN_COL_BLOCK
    t_idx = pl.program_id(1)

    @pl.when(t_idx == 0)
    def _():
        halo_s[...] = jnp.zeros(halo_s.shape, _F32)

    mod = mod_ref[0, 0]
    sh2 = mod[:, 3 * D:4 * D]
    sc2 = mod[:, 4 * D:5 * D]
    g2 = mod[:, 5 * D:6 * D]
    pre_gain = pre_g_ref[...] * (1.0 + sc2)
    post_gain = g2 * post_g_ref[...]
    off = QK_HALO - (FFN_KERNEL - 1)
    n_lt = W // LANES

    x = x_ref[0]
    u_s[...] = ((x * _rms_scale(x)) * pre_gain + sh2).astype(_BF16)

    def up_conv(slot, part, col0):
        res = _dot(u_s[...], up_ref[0, :, col0:col0 + W])
        out = []
        for jj in range(n_lt):
            lt = col0 // LANES + jj
            cols = slice(lt * LANES, (lt + 1) * LANES)
            buf = buf_s.at[slot, part, jj]
            buf[0:QK_HALO, :] = halo_s[lt]
            buf[QK_HALO:QK_HALO + T, :] = res[:, jj * LANES:(jj + 1) * LANES]
            halo_s[lt] = buf[T:T + QK_HALO, :]
            acc = cb_ref[:, cols] + cw_ref[0:1, cols] * buf[off:off + T, :]
            for k in range(1, FFN_KERNEL):
                acc = acc + cw_ref[k:k + 1, cols] * buf[off + k:off + k + T, :]
            out.append(acc)
        return out

    for j in range(D_FF // W):
        a = up_conv(j % 2, 0, j * W)
        gt = up_conv(j % 2, 1, D_FF + j * W)
        for jj in range(n_lt):
            f_s[:, j * W + jj * LANES:j * W + (jj + 1) * LANES] = (
                _silu(gt[jj]) * a[jj]).astype(_BF16)

    f = _dot(f_s[...], down_ref[0])
    out_ref[0] = x + (f * _rms_scale(f)) * post_gain


def _ffn(l, x, mod_all, pre_g, post_g, up, conv_w, conv_b, down):
    bsz, seq, d = x.shape
    T = SEQ_TILE
    cw8 = jnp.pad(conv_w, ((0, SUBLANES - FFN_KERNEL), (0, 0)))
    operands = [
        (x, pl.BlockSpec((1, T, d), lambda b, t: (b, t, 0))),
        (mod_all, pl.BlockSpec((1, 1, 1, N_MOD * d), lambda b, t: (l, b, 0, 0))),
        (pre_g.reshape(1, d), None),
        (post_g.reshape(1, d), None),
        (up, _layer_spec(up, l)),
        (cw8, None),
        (conv_b.reshape(1, 2 * D_FF), None),
        (down, _layer_spec(down, l)),
    ]
    args = [a for a, _ in operands]
    in_specs = [s if s is not None else _const_spec(a.shape) for a, s in operands]
    scratch = [
        pltpu.VMEM((T, d), _BF16),
        pltpu.VMEM((2, 2, FFN_COL_BLOCK // LANES, QK_HALO + T, LANES), _F32),
        pltpu.VMEM((2 * D_FF // LANES, QK_HALO, LANES), _F32),
        pltpu.VMEM((T, D_FF), _BF16),
    ]
    return pl.pallas_call(
        _ffn_kernel,
        grid=(bsz, seq // T),
        in_specs=in_specs,
        out_specs=pl.BlockSpec((1, T, d), lambda b, t: (b, t, 0)),
        out_shape=jax.ShapeDtypeStruct(x.shape, x.dtype),
        scratch_shapes=scratch,
        compiler_params=pltpu.CompilerParams(
            dimension_semantics=("arbitrary", "arbitrary"),
            vmem_limit_bytes=VMEM_LIMIT_BYTES),
        name="ffn",
    )(*args)


def kernel(x, c, ada_w, ada_b, mix_pre_g, mix_post_g, w_in, qk_conv_w, qk_conv_b, igate_b, fgate_b, ml_norm_g, cv_dw_w, cv_dw_b, cv_ln_g, cv_ln_b, w_out, ffn_pre_g, ffn_post_g, ffn_up, ffn_conv_w, ffn_conv_b, ffn_down):
    bsz, seq, d = x.shape
    depth = ada_w.shape[0]
    assert d == D_MODEL and seq % SEQ_TILE == 0
    mod_all = _modulation(c, ada_w, ada_b).reshape(depth, bsz, 1, N_MOD * d)
    w_main = w_in.astype(_BF16)
    w_glu = w_main[:, :, GATE_COL + N_GATE:]
    w_out16 = w_out.astype(_BF16)
    up16 = ffn_up.astype(_BF16)
    down16 = ffn_down.astype(_BF16)
    for l in range(depth):
        x = _mixer(l, x, mod_all, mix_pre_g[l], mix_post_g[l], w_main, w_glu, qk_conv_w[l],
                   qk_conv_b[l], igate_b[l], fgate_b[l], ml_norm_g[l], cv_dw_w[l], cv_dw_b[l],
                   cv_ln_g[l], cv_ln_b[l], w_out16)
        x = _ffn(l, x, mod_all, ffn_pre_g[l], ffn_post_g[l], up16, ffn_conv_w[l],
                 ffn_conv_b[l], down16)
    return x
```

```python
import jax
import jax.numpy as jnp
from jax import lax
from jax.experimental import pallas as pl
from jax.experimental.pallas import tpu as pltpu

D_MODEL = 1024
N_HEADS = 4
HEAD_DIM = 256
D_ML = N_HEADS * HEAD_DIM
D_CV = 1024
QK_CONV = 4
CV_KERNEL = 31
ML_CHUNK = 128
D_FF = 2816
FFN_KERNEL = 3
N_MOD = 6
N_GATE = 2 * N_HEADS
EPS = 1e-6

LANES = 128
SUBLANES = 8
SEQ_TILE = 512
QK_HALO = SUBLANES
CV_HALO = 4 * SUBLANES
CV_ROWS = 8
FFN_COL_BLOCK = 256
VMEM_LIMIT_BYTES = 58 * 1024 * 1024

LT_PER_HEAD = HEAD_DIM // LANES
N_CV_LT = D_CV // LANES
GATE_COL = 4 * D_ML
GLU_COL_BLOCK = 512

_F32 = jnp.float32
_BF16 = jnp.bfloat16
_NEG_INF = float("-inf")


def _dot(a, b):
    return jnp.dot(a, b, preferred_element_type=_F32)


def _sigmoid(x):
    return 0.5 * jnp.tanh(0.5 * x) + 0.5


def _silu(x):
    h = 0.5 * x
    return h * jnp.tanh(h) + h


def _rms_scale(x):
    return lax.rsqrt(jnp.mean(x * x, axis=-1, keepdims=True) + EPS)


def _mod_kernel(c_ref, w_ref, b_ref, o_ref):
    cond = _silu(c_ref[...]).astype(_BF16)
    o_ref[0] = _dot(cond, w_ref[0].astype(_BF16)) + b_ref[0]


def _modulation(c, ada_w, ada_b):
    depth, d, n = ada_w.shape
    bsz = c.shape[0]
    nb = n // d
    return pl.pallas_call(
        _mod_kernel,
        grid=(depth, nb),
        in_specs=[
            pl.BlockSpec((bsz, d), lambda l, j: (0, 0)),
            pl.BlockSpec((1, d, d), lambda l, j: (l, 0, j)),
            pl.BlockSpec((1, 1, d), lambda l, j: (l, 0, j)),
        ],
        out_specs=pl.BlockSpec((1, bsz, d), lambda l, j: (l, 0, j)),
        out_shape=jax.ShapeDtypeStruct((depth, bsz, n), _F32),
        compiler_params=pltpu.CompilerParams(
            dimension_semantics=("arbitrary", "arbitrary")),
        name="modulation",
    )(c, ada_w, ada_b.reshape(depth, 1, n))


def _mixer_kernel(x_ref, mod_ref, pre_g_ref, post_g_ref, w_main_ref, w_glu_ref, gate_b_ref,
                  qkc_w_ref, qkc_b_ref, mln_g_ref, cv_w_ref, cv_b_ref, ln_g_ref, ln_b_ref,
                  w_out_ref, out_ref,
                  u_s, proj_buf, q_s, kt_s, cv_buf, y_s, hy_s, cn_s, m_s):
    T = SEQ_TILE
    D = D_MODEL
    L = ML_CHUNK
    t_idx = pl.program_id(1)

    @pl.when(t_idx == 0)
    def _():
        proj_buf[:, 0:QK_HALO, :] = jnp.zeros((proj_buf.shape[0], QK_HALO, LANES), _F32)
        cv_buf[:, 0:CV_HALO, :] = jnp.zeros((cv_buf.shape[0], CV_HALO, LANES), _F32)
        cn_s[...] = jnp.zeros(cn_s.shape, _F32)
        m_s[...] = jnp.zeros(m_s.shape, _F32)

    x = x_ref[0]
    mod = mod_ref[0, 0]
    sh1 = mod[:, 0:D]
    sc1 = mod[:, D:2 * D]
    g1 = mod[:, 2 * D:3 * D]
    u = (x * _rms_scale(x)) * (pre_g_ref[...] * (1.0 + sc1)) + sh1
    u_s[...] = u.astype(_BF16)

    for jb in range(D_CV // GLU_COL_BLOCK):
        c0 = jb * GLU_COL_BLOCK
        a = _dot(u_s[...], w_glu_ref[0, :, c0:c0 + GLU_COL_BLOCK])
        gt = _dot(u_s[...], w_glu_ref[0, :, c0 + D_CV:c0 + D_CV + GLU_COL_BLOCK])
        glu = a * _sigmoid(gt)
        for jj in range(GLU_COL_BLOCK // LANES):
            cv_buf[jb * (GLU_COL_BLOCK // LANES) + jj, CV_HALO:CV_HALO + T, :] = (
                glu[:, jj * LANES:(jj + 1) * LANES])
    cv_off = CV_HALO - (CV_KERNEL - 1)
    for j in range(N_CV_LT):
        for i in range(T // CV_ROWS):
            r0 = i * CV_ROWS + cv_off
            acc = cv_b_ref[j] + cv_w_ref[j, 0:1, :] * cv_buf[j, r0:r0 + CV_ROWS, :]
            for k in range(1, CV_KERNEL):
                acc = acc + cv_w_ref[j, k:k + 1, :] * cv_buf[j, r0 + k:r0 + k + CV_ROWS, :]
            y_s[j, i * CV_ROWS:(i + 1) * CV_ROWS, :] = acc
    cv_buf[:, 0:CV_HALO, :] = cv_buf[:, T:T + CV_HALO, :]

    tot = y_s[0]
    for j in range(1, N_CV_LT):
        tot = tot + y_s[j]
    mu = jnp.broadcast_to(jnp.sum(tot, axis=-1, keepdims=True) * (1.0 / D_CV), (T, LANES))
    sq = jnp.square(y_s[0] - mu)
    for j in range(1, N_CV_LT):
        sq = sq + jnp.square(y_s[j] - mu)
    var = jnp.sum(sq, axis=-1, keepdims=True) * (1.0 / D_CV)
    inv = jnp.broadcast_to(lax.rsqrt(var + EPS), (T, LANES))
    for j in range(N_CV_LT):
        lcols = slice(j * LANES, (j + 1) * LANES)
        yn = (y_s[j] - mu) * inv * ln_g_ref[:, lcols] + ln_b_ref[:, lcols]
        hy_s[:, D_ML + j * LANES:D_ML + (j + 1) * LANES] = _silu(yn).astype(_BF16)

    gates = _dot(u_s[...], w_main_ref[0, :, GATE_COL:GATE_COL + LANES]) + gate_b_ref[...]
    lane_t = lax.broadcasted_iota(jnp.int32, (T, LANES), 1)
    log_f = jnp.minimum(gates, 0.0) - jnp.log1p(jnp.exp(-jnp.abs(gates)))
    z = jnp.where(lane_t < N_HEADS, gates, log_f)
    row_i = lax.broadcasted_iota(jnp.int32, (L, L), 0)
    col_i = lax.broadcasted_iota(jnp.int32, (L, L), 1)
    causal = row_i >= col_i
    tril = causal.astype(_F32)
    lane_l = lax.broadcasted_iota(jnp.int32, (L, LANES), 1)
    zb_chunks, zbt_chunks = [], []
    for c in range(T // L):
        zc = z[c * L:(c + 1) * L, :]
        csum = jnp.dot(tril, zc, precision=lax.Precision.HIGHEST, preferred_element_type=_F32)
        zb = jnp.where(lane_l < N_HEADS, zc, csum)
        zb_chunks.append(zb)
        zbt_chunks.append(zb.T)

    off = QK_HALO - (QK_CONV - 1)

    def qk_conv(slab, col0):
        cols = slice(col0, col0 + LANES)
        acc = qkc_b_ref[:, cols] + qkc_w_ref[0:1, cols] * proj_buf[slab, off:off + T, :]
        for k in range(1, QK_CONV):
            acc = acc + qkc_w_ref[k:k + 1, cols] * proj_buf[slab, off + k:off + k + T, :]
        return _silu(acc)

    ones_blk = jnp.ones((L, LANES), _BF16)
    mix = None
    for h in range(N_HEADS):
        hc = slice(h * HEAD_DIM, (h + 1) * HEAD_DIM)
        slab_q = h * 4 * LT_PER_HEAD
        slab_k = slab_q + LT_PER_HEAD
        slab_v = slab_k + LT_PER_HEAD
        slab_o = slab_v + LT_PER_HEAD
        for g in range(4):
            c0 = g * D_ML + h * HEAD_DIM
            res = _dot(u_s[...], w_main_ref[0, :, c0:c0 + HEAD_DIM])
            for jj in range(LT_PER_HEAD):
                proj_buf[slab_q + g * LT_PER_HEAD + jj, QK_HALO:QK_HALO + T, :] = (
                    res[:, jj * LANES:(jj + 1) * LANES])
        for jj in range(LT_PER_HEAD):
            lt = h * LT_PER_HEAD + jj
            q_s[:, lt * LANES:(lt + 1) * LANES] = qk_conv(slab_q + jj, lt * LANES).astype(_BF16)
            kt_s[lt * LANES:(lt + 1) * LANES, :] = (
                qk_conv(slab_k + jj, D_ML + lt * LANES) * (HEAD_DIM ** -0.5)).T
        proj_buf[slab_q:slab_v, 0:QK_HALO, :] = proj_buf[slab_q:slab_v, T:T + QK_HALO, :]

        cn = cn_s[h]
        m_prev = m_s[h, 0:1, :]
        for c in range(T // L):
            r = slice(c * L, (c + 1) * L)
            pr = slice(QK_HALO + c * L, QK_HALO + (c + 1) * L)
            zb, zbt = zb_chunks[c], zbt_chunks[c]
            b_rep = jnp.broadcast_to(zb[:, N_HEADS + h:N_HEADS + h + 1], (L, LANES))
            i_row = zbt[h:h + 1, :]
            b_row = zbt[N_HEADS + h:N_HEADS + h + 1, :]
            q = q_s[r, hc]
            kt = kt_s[hc, r]
            v_aug = jnp.concatenate(
                [proj_buf[slab_v + jj, pr, :].astype(_BF16)
                 for jj in range(LT_PER_HEAD)] + [ones_blk], axis=1)
            dmat = jnp.where(causal, b_rep + (i_row - b_row), _NEG_INF)
            bm = b_rep + m_prev
            m_t = jnp.maximum(bm, jnp.broadcast_to(jnp.max(dmat, axis=-1, keepdims=True), (L, LANES)))
            s = _dot(q, kt.astype(_BF16)) * jnp.exp(dmat - m_t)
            intra = _dot(s.astype(_BF16), v_aug)
            inter = _dot(q, cn.astype(_BF16))
            w_inter = jnp.exp(bm - m_t)
            comb = [w_inter * inter[:, jj * LANES:(jj + 1) * LANES] + intra[:, jj * LANES:(jj + 1) * LANES]
                    for jj in range(LT_PER_HEAD + 1)]
            den = comb[LT_PER_HEAD]
            inv_den = 1.0 / jnp.maximum(jnp.abs(den), jnp.exp(-m_t))
            hh = [comb[jj] * inv_den for jj in range(LT_PER_HEAD)]
            hsq = hh[0] * hh[0]
            for jj in range(1, LT_PER_HEAD):
                hsq = hsq + hh[jj] * hh[jj]
            rs = jnp.broadcast_to(
                lax.rsqrt(jnp.sum(hsq, axis=-1, keepdims=True) * (1.0 / HEAD_DIM) + EPS), (L, LANES))
            for jj in range(LT_PER_HEAD):
                cols = slice(h * HEAD_DIM + jj * LANES, h * HEAD_DIM + (jj + 1) * LANES)
                o_pre = proj_buf[slab_o + jj, pr, :]
                hy_s[r, cols] = (hh[jj] * rs * mln_g_ref[:, cols] * _sigmoid(o_pre)).astype(_BF16)
            b_last = b_rep[L - 1:L, :]
            g_row = b_last - b_row + i_row
            m_new = jnp.maximum(
                b_last + m_prev,
                jnp.broadcast_to(jnp.max(g_row, axis=-1, keepdims=True), (1, LANES)))
            kwt = kt * jnp.exp(g_row - m_new)
            upd = _dot(kwt.astype(_BF16), v_aug)
            decay = jnp.exp(b_last + m_prev - m_new)
            cn = jnp.concatenate(
                [decay * cn[:, jj * LANES:(jj + 1) * LANES] + upd[:, jj * LANES:(jj + 1) * LANES]
                 for jj in range(LT_PER_HEAD + 1)], axis=1)
            m_prev = m_new
        cn_s[h] = cn
        m_s[h] = jnp.broadcast_to(m_prev, (SUBLANES, LANES))
        part = _dot(hy_s[:, hc], w_out_ref[0, hc, :])
        mix = part if mix is None else mix + part
    mix = mix + _dot(hy_s[:, D_ML:D_ML + D_CV], w_out_ref[0, D_ML:D_ML + D_CV, :])

    out_ref[0] = x + (mix * _rms_scale(mix)) * (g1 * post_g_ref[...])


def _const_spec(shape, index=None):
    index = index if index is not None else (0,) * len(shape)
    return pl.BlockSpec(shape, lambda b, t: index, pipeline_mode=pl.Buffered(1))


def _layer_spec(arr, l):
    return _const_spec((1,) + arr.shape[1:], (l,) + (0,) * (arr.ndim - 1))


def _mixer(l, x, mod_all, pre_g, post_g, w_main, w_glu, qkc_w, qkc_b, igate_b, fgate_b, mln_g,
           cv_w, cv_b, ln_g, ln_b, w_out):
    bsz, seq, d = x.shape
    T = SEQ_TILE
    gate_b = jnp.pad(jnp.concatenate([igate_b, fgate_b]), (0, LANES - N_GATE)).reshape(1, LANES)
    cv_w3 = jnp.pad(cv_w, ((0, 4 * SUBLANES - CV_KERNEL), (0, 0)))
    cv_w3 = cv_w3.reshape(4 * SUBLANES, N_CV_LT, LANES).transpose(1, 0, 2)
    cv_b3 = cv_b.reshape(N_CV_LT, 1, LANES)
    qkc_w8 = jnp.pad(qkc_w, ((0, SUBLANES - QK_CONV), (0, 0)))

    operands = [
        (x, pl.BlockSpec((1, T, d), lambda b, t: (b, t, 0))),
        (mod_all, pl.BlockSpec((1, 1, 1, N_MOD * d), lambda b, t: (l, b, 0, 0))),
        (pre_g.reshape(1, d), None),
        (post_g.reshape(1, d), None),
        (w_main, _layer_spec(w_main, l)),
        (w_glu, _layer_spec(w_glu, l)),
        (gate_b, None),
        (qkc_w8, None), (qkc_b.reshape(1, 2 * D_ML), None), (mln_g.reshape(1, D_ML), None),
        (cv_w3, None), (cv_b3, None), (ln_g.reshape(1, D_CV), None), (ln_b.reshape(1, D_CV), None),
        (w_out, _layer_spec(w_out, l)),
    ]
    args = [a for a, _ in operands]
    in_specs = [s if s is not None else _const_spec(a.shape) for a, s in operands]
    scratch = [
        pltpu.VMEM((T, d), _BF16),
        pltpu.VMEM((4 * D_ML // LANES, QK_HALO + T, LANES), _F32),
        pltpu.VMEM((T, D_ML), _BF16),
        pltpu.VMEM((D_ML, T), _F32),
        pltpu.VMEM((N_CV_LT, CV_HALO + T, LANES), _F32),
        pltpu.VMEM((N_CV_LT, T, LANES), _F32),
        pltpu.VMEM((T, D_ML + D_CV), _BF16),
        pltpu.VMEM((N_HEADS, HEAD_DIM, HEAD_DIM + LANES), _F32),
        pltpu.VMEM((N_HEADS, SUBLANES, LANES), _F32),
    ]
    return pl.pallas_call(
        _mixer_kernel,
        grid=(bsz, seq // T),
        in_specs=in_specs,
        out_specs=pl.BlockSpec((1, T, d), lambda b, t: (b, t, 0)),
        out_shape=jax.ShapeDtypeStruct(x.shape, x.dtype),
        scratch_shapes=scratch,
        compiler_params=pltpu.CompilerParams(
            dimension_semantics=("arbitrary", "arbitrary"),
            vmem_limit_bytes=VMEM_LIMIT_BYTES),
        name="mixer",
    )(*args)


def _ffn_kernel(x_ref, mod_ref, pre_g_ref, post_g_ref, up_ref, cw_ref, cb_ref, down_ref,
                out_ref, u_s, buf_s, halo_s, f_s):
    T = SEQ_TILE
    D = D_MODEL
    W = FFN_COL_BLOCK
    t_idx = pl.program_id(1)

    @pl.when(t_idx == 0)
    def _():
        halo_s[...] = jnp.zeros(halo_s.shape, _F32)

    mod = mod_ref[0, 0]
    sh2 = mod[:, 3 * D:4 * D]
    sc2 = mod[:, 4 * D:5 * D]
    g2 = mod[:, 5 * D:6 * D]
    pre_gain = pre_g_ref[...] * (1.0 + sc2)
    post_gain = g2 * post_g_ref[...]
    off = QK_HALO - (FFN_KERNEL - 1)
    n_lt = W // LANES

    x = x_ref[0]
    u_s[...] = ((x * _rms_scale(x)) * pre_gain + sh2).astype(_BF16)

    def up_conv(slot, part, col0):
        res = _dot(u_s[...], up_ref[0, :, col0:col0 + W])
        out = []
        for jj in range(n_lt):
            lt = col0 // LANES + jj
            cols = slice(lt * LANES, (lt + 1) * LANES)
            buf = buf_s.at[slot, part, jj]
            buf[0:QK_HALO, :] = halo_s[lt]
            buf[QK_HALO:QK_HALO + T, :] = res[:, jj * LANES:(jj + 1) * LANES]
            halo_s[lt] = buf[T:T + QK_HALO, :]
            acc = cb_ref[:, cols] + cw_ref[0:1, cols] * buf[off:off + T, :]
            for k in range(1, FFN_KERNEL):
                acc = acc + cw_ref[k:k + 1, cols] * buf[off + k:off + k + T, :]
            out.append(acc)
        return out

    for j in range(D_FF // W):
        a = up_conv(j % 2, 0, j * W)
        gt = up_conv(j % 2, 1, D_FF + j * W)
        for jj in range(n_lt):
            f_s[:, j * W + jj * LANES:j * W + (jj + 1) * LANES] = (
                _silu(gt[jj]) * a[jj]).astype(_BF16)

    f = _dot(f_s[...], down_ref[0])
    out_ref[0] = x + (f * _rms_scale(f)) * post_gain


def _ffn(l, x, mod_all, pre_g, post_g, up, conv_w, conv_b, down):
    bsz, seq, d = x.shape
    T = SEQ_TILE
    cw8 = jnp.pad(conv_w, ((0, SUBLANES - FFN_KERNEL), (0, 0)))
    operands = [
        (x, pl.BlockSpec((1, T, d), lambda b, t: (b, t, 0))),
        (mod_all, pl.BlockSpec((1, 1, 1, N_MOD * d), lambda b, t: (l, b, 0, 0))),
        (pre_g.reshape(1, d), None),
        (post_g.reshape(1, d), None),
        (up, _layer_spec(up, l)),
        (cw8, None),
        (conv_b.reshape(1, 2 * D_FF), None),
        (down, _layer_spec(down, l)),
    ]
    args = [a for a, _ in operands]
    in_specs = [s if s is not None else _const_spec(a.shape) for a, s in operands]
    scratch = [
        pltpu.VMEM((T, d), _BF16),
        pltpu.VMEM((2, 2, FFN_COL_BLOCK // LANES, QK_HALO + T, LANES), _F32),
        pltpu.VMEM((2 * D_FF // LANES, QK_HALO, LANES), _F32),
        pltpu.VMEM((T, D_FF), _BF16),
    ]
    return pl.pallas_call(
        _ffn_kernel,
        grid=(bsz, seq // T),
        in_specs=in_specs,
        out_specs=pl.BlockSpec((1, T, d), lambda b, t: (b, t, 0)),
        out_shape=jax.ShapeDtypeStruct(x.shape, x.dtype),
        scratch_shapes=scratch,
        compiler_params=pltpu.CompilerParams(
            dimension_semantics=("arbitrary", "arbitrary"),
            vmem_limit_bytes=VMEM_LIMIT_BYTES),
        name="ffn",
    )(*args)


def kernel(x, c, ada_w, ada_b, mix_pre_g, mix_post_g, w_in, qk_conv_w, qk_conv_b, igate_b, fgate_b, ml_norm_g, cv_dw_w, cv_dw_b, cv_ln_g, cv_ln_b, w_out, ffn_pre_g, ffn_post_g, ffn_up, ffn_conv_w, ffn_conv_b, ffn_down):
    bsz, seq, d = x.shape
    depth = ada_w.shape[0]
    assert d == D_MODEL and seq % SEQ_TILE == 0
    mod_all = _modulation(c, ada_w, ada_b).reshape(depth, bsz, 1, N_MOD * d)
    w_main = w_in.astype(_BF16)
    w_glu = w_main[:, :, GATE_COL + N_GATE:]
    w_out16 = w_out.astype(_BF16)
    up16 = ffn_up.astype(_BF16)
    down16 = ffn_down.astype(_BF16)
    for l in range(depth):
        x = _mixer(l, x, mod_all, mix_pre_g[l], mix_post_g[l], w_main, w_glu, qk_conv_w[l],
                   qk_conv_b[l], igate_b[l], fgate_b[l], ml_norm_g[l], cv_dw_w[l], cv_dw_b[l],
                   cv_ln_g[l], cv_ln_b[l], w_out16)
        x = _ffn(l, x, mod_all, ffn_pre_g[l], ffn_post_g[l], up16, ffn_conv_w[l],
                 ffn_conv_b[l], down16)
    return x
```

```python
import jax
import jax.numpy as jnp
from jax import lax
from jax.experimental import pallas as pl
from jax.experimental.pallas import tpu as pltpu

D_MODEL = 1024
N_HEADS = 4
HEAD_DIM = 256
D_ML = N_HEADS * HEAD_DIM
D_CV = 1024
QK_CONV = 4
CV_KERNEL = 31
ML_CHUNK = 128
D_FF = 2816
FFN_KERNEL = 3
N_MOD = 6
N_GATE = 2 * N_HEADS
EPS = 1e-6

LANES = 128
SUBLANES = 8
SEQ_TILE = 512
QK_HALO = SUBLANES
CV_HALO = 4 * SUBLANES
CV_ROWS = 16
FFN_COL_BLOCK = 256
VMEM_LIMIT_BYTES = 58 * 1024 * 1024

LT_PER_HEAD = HEAD_DIM // LANES
N_CV_LT = D_CV // LANES
GATE_COL = 4 * D_ML
GLU_COL_BLOCK = 512

_F32 = jnp.float32
_BF16 = jnp.bfloat16
_NEG_INF = float("-inf")


def _dot(a, b):
    return jnp.dot(a, b, preferred_element_type=_F32)


def _sigmoid(x):
    return 0.5 * jnp.tanh(0.5 * x) + 0.5


def _silu(x):
    h = 0.5 * x
    return h * jnp.tanh(h) + h


def _rms_scale(x):
    return lax.rsqrt(jnp.mean(x * x, axis=-1, keepdims=True) + EPS)


def _mod_kernel(c_ref, w_ref, b_ref, o_ref):
    cond = _silu(c_ref[...]).astype(_BF16)
    o_ref[0] = _dot(cond, w_ref[0].astype(_BF16)) + b_ref[0]


def _modulation(c, ada_w, ada_b):
    depth, d, n = ada_w.shape
    bsz = c.shape[0]
    nb = n // d
    return pl.pallas_call(
        _mod_kernel,
        grid=(depth, nb),
        in_specs=[
            pl.BlockSpec((bsz, d), lambda l, j: (0, 0)),
            pl.BlockSpec((1, d, d), lambda l, j: (l, 0, j)),
            pl.BlockSpec((1, 1, d), lambda l, j: (l, 0, j)),
        ],
        out_specs=pl.BlockSpec((1, bsz, d), lambda l, j: (l, 0, j)),
        out_shape=jax.ShapeDtypeStruct((depth, bsz, n), _F32),
        compiler_params=pltpu.CompilerParams(
            dimension_semantics=("arbitrary", "arbitrary")),
        name="modulation",
    )(c, ada_w, ada_b.reshape(depth, 1, n))


def _mixer_kernel(x_ref, mod_ref, pre_g_ref, post_g_ref, w_main_ref, w_glu_ref, gate_b_ref,
                  qkc_w_ref, qkc_b_ref, mln_g_ref, cv_w_ref, cv_b_ref, ln_g_ref, ln_b_ref,
                  w_out_ref, out_ref,
                  u_s, proj_buf, q_s, kt_s, cv_buf, y_s, hy_s, cn_s, m_s):
    T = SEQ_TILE
    D = D_MODEL
    L = ML_CHUNK
    t_idx = pl.program_id(1)

    @pl.when(t_idx == 0)
    def _():
        proj_buf[:, 0:QK_HALO, :] = jnp.zeros((proj_buf.shape[0], QK_HALO, LANES), _F32)
        cv_buf[:, 0:CV_HALO, :] = jnp.zeros((cv_buf.shape[0], CV_HALO, LANES), _F32)
        cn_s[...] = jnp.zeros(cn_s.shape, _F32)
        m_s[...] = jnp.zeros(m_s.shape, _F32)

    x = x_ref[0]
    mod = mod_ref[0, 0]
    sh1 = mod[:, 0:D]
    sc1 = mod[:, D:2 * D]
    g1 = mod[:, 2 * D:3 * D]
    u = (x * _rms_scale(x)) * (pre_g_ref[...] * (1.0 + sc1)) + sh1
    u_s[...] = u.astype(_BF16)

    for jb in range(D_CV // GLU_COL_BLOCK):
        c0 = jb * GLU_COL_BLOCK
        a = _dot(u_s[...], w_glu_ref[0, :, c0:c0 + GLU_COL_BLOCK])
        gt = _dot(u_s[...], w_glu_ref[0, :, c0 + D_CV:c0 + D_CV + GLU_COL_BLOCK])
        glu = a * _sigmoid(gt)
        for jj in range(GLU_COL_BLOCK // LANES):
            cv_buf[jb * (GLU_COL_BLOCK // LANES) + jj, CV_HALO:CV_HALO + T, :] = (
                glu[:, jj * LANES:(jj + 1) * LANES])
    cv_off = CV_HALO - (CV_KERNEL - 1)
    for j in range(N_CV_LT):
        for i in range(T // CV_ROWS):
            r0 = i * CV_ROWS + cv_off
            acc = cv_b_ref[j] + cv_w_ref[j, 0:1, :] * cv_buf[j, r0:r0 + CV_ROWS, :]
            for k in range(1, CV_KERNEL):
                acc = acc + cv_w_ref[j, k:k + 1, :] * cv_buf[j, r0 + k:r0 + k + CV_ROWS, :]
            y_s[j, i * CV_ROWS:(i + 1) * CV_ROWS, :] = acc
    cv_buf[:, 0:CV_HALO, :] = cv_buf[:, T:T + CV_HALO, :]

    tot = y_s[0]
    for j in range(1, N_CV_LT):
        tot = tot + y_s[j]
    mu = jnp.broadcast_to(jnp.sum(tot, axis=-1, keepdims=True) * (1.0 / D_CV), (T, LANES))
    sq = jnp.square(y_s[0] - mu)
    for j in range(1, N_CV_LT):
        sq = sq + jnp.square(y_s[j] - mu)
    var = jnp.sum(sq, axis=-1, keepdims=True) * (1.0 / D_CV)
    inv = jnp.broadcast_to(lax.rsqrt(var + EPS), (T, LANES))
    for j in range(N_CV_LT):
        lcols = slice(j * LANES, (j + 1) * LANES)
        yn = (y_s[j] - mu) * inv * ln_g_ref[:, lcols] + ln_b_ref[:, lcols]
        hy_s[:, D_ML + j * LANES:D_ML + (j + 1) * LANES] = _silu(yn).astype(_BF16)

    gates = _dot(u_s[...], w_main_ref[0, :, GATE_COL:GATE_COL + LANES]) + gate_b_ref[...]
    lane_t = lax.broadcasted_iota(jnp.int32, (T, LANES), 1)
    log_f = jnp.minimum(gates, 0.0) - jnp.log1p(jnp.exp(-jnp.abs(gates)))
    z = jnp.where(lane_t < N_HEADS, gates, log_f)
    row_i = lax.broadcasted_iota(jnp.int32, (L, L), 0)
    col_i = lax.broadcasted_iota(jnp.int32, (L, L), 1)
    causal = row_i >= col_i
    tril = causal.astype(_F32)
    lane_l = lax.broadcasted_iota(jnp.int32, (L, LANES), 1)
    zb_chunks, zbt_chunks = [], []
    for c in range(T // L):
        zc = z[c * L:(c + 1) * L, :]
        csum = jnp.dot(tril, zc, precision=lax.Precision.HIGHEST, preferred_element_type=_F32)
        zb = jnp.where(lane_l < N_HEADS, zc, csum)
        zb_chunks.append(zb)
        zbt_chunks.append(zb.T)

    off = QK_HALO - (QK_CONV - 1)

    def qk_conv(slab, col0, r0):
        cols = slice(col0, col0 + LANES)
        acc = qkc_b_ref[:, cols] + qkc_w_ref[0:1, cols] * proj_buf[slab, r0 + off:r0 + off + L, :]
        for k in range(1, QK_CONV):
            acc = acc + qkc_w_ref[k:k + 1, cols] * proj_buf[slab, r0 + off + k:r0 + off + k + L, :]
        return _silu(acc)

    ones_blk = jnp.ones((L, LANES), _BF16)
    mix = None
    for h in range(N_HEADS):
        hc = slice(h * HEAD_DIM, (h + 1) * HEAD_DIM)
        slab_q = h * 4 * LT_PER_HEAD
        slab_k = slab_q + LT_PER_HEAD
        slab_v = slab_k + LT_PER_HEAD
        slab_o = slab_v + LT_PER_HEAD
        for g in range(4):
            c0 = g * D_ML + h * HEAD_DIM
            res = _dot(u_s[...], w_main_ref[0, :, c0:c0 + HEAD_DIM])
            for jj in range(LT_PER_HEAD):
                proj_buf[slab_q + g * LT_PER_HEAD + jj, QK_HALO:QK_HALO + T, :] = (
                    res[:, jj * LANES:(jj + 1) * LANES])
        for jj in range(LT_PER_HEAD):
            lt = h * LT_PER_HEAD + jj
            for r0 in range(0, T, L):
                q_s[r0:r0 + L, lt * LANES:(lt + 1) * LANES] = (
                    qk_conv(slab_q + jj, lt * LANES, r0).astype(_BF16))
                kt_s[lt * LANES:(lt + 1) * LANES, r0:r0 + L] = (
                    qk_conv(slab_k + jj, D_ML + lt * LANES, r0) * (HEAD_DIM ** -0.5)).T
        proj_buf[slab_q:slab_v, 0:QK_HALO, :] = proj_buf[slab_q:slab_v, T:T + QK_HALO, :]

        cn = cn_s[h]
        m_prev = m_s[h, 0:1, :]
        for c in range(T // L):
            r = slice(c * L, (c + 1) * L)
            pr = slice(QK_HALO + c * L, QK_HALO + (c + 1) * L)
            zb, zbt = zb_chunks[c], zbt_chunks[c]
            b_rep = jnp.broadcast_to(zb[:, N_HEADS + h:N_HEADS + h + 1], (L, LANES))
            i_row = zbt[h:h + 1, :]
            b_row = zbt[N_HEADS + h:N_HEADS + h + 1, :]
            q = q_s[r, hc]
            kt = kt_s[hc, r]
            v_aug = jnp.concatenate(
                [proj_buf[slab_v + jj, pr, :].astype(_BF16)
                 for jj in range(LT_PER_HEAD)] + [ones_blk], axis=1)
            dmat = jnp.where(causal, b_rep + (i_row - b_row), _NEG_INF)
            bm = b_rep + m_prev
            m_t = jnp.maximum(bm, jnp.broadcast_to(jnp.max(dmat, axis=-1, keepdims=True), (L, LANES)))
            s = _dot(q, kt.astype(_BF16)) * jnp.exp(dmat - m_t)
            intra = _dot(s.astype(_BF16), v_aug)
            inter = _dot(q, cn.astype(_BF16))
            w_inter = jnp.exp(bm - m_t)
            comb = [w_inter * inter[:, jj * LANES:(jj + 1) * LANES] + intra[:, jj * LANES:(jj + 1) * LANES]
                    for jj in range(LT_PER_HEAD + 1)]
            den = comb[LT_PER_HEAD]
            inv_den = 1.0 / jnp.maximum(jnp.abs(den), jnp.exp(-m_t))
            hh = [comb[jj] * inv_den for jj in range(LT_PER_HEAD)]
            hsq = hh[0] * hh[0]
            for jj in range(1, LT_PER_HEAD):
                hsq = hsq + hh[jj] * hh[jj]
            rs = jnp.broadcast_to(
                lax.rsqrt(jnp.sum(hsq, axis=-1, keepdims=True) * (1.0 / HEAD_DIM) + EPS), (L, LANES))
            for jj in range(LT_PER_HEAD):
                cols = slice(h * HEAD_DIM + jj * LANES, h * HEAD_DIM + (jj + 1) * LANES)
                o_pre = proj_buf[slab_o + jj, pr, :]
                hy_s[r, cols] = (hh[jj] * rs * mln_g_ref[:, cols] * _sigmoid(o_pre)).astype(_BF16)
            b_last = b_rep[L - 1:L, :]
            g_row = b_last - b_row + i_row
            m_new = jnp.maximum(
                b_last + m_prev,
                jnp.broadcast_to(jnp.max(g_row, axis=-1, keepdims=True), (1, LANES)))
            kwt = kt * jnp.exp(g_row - m_new)
            upd = _dot(kwt.astype(_BF16), v_aug)
            decay = jnp.exp(b_last + m_prev - m_new)
            cn = jnp.concatenate(
                [decay * cn[:, jj * LANES:(jj + 1) * LANES] + upd[:, jj * LANES:(jj + 1) * LANES]
                 for jj in range(LT_PER_HEAD + 1)], axis=1)
            m_prev = m_new
        cn_s[h] = cn
        m_s[h] = jnp.broadcast_to(m_prev, (SUBLANES, LANES))
        part = _dot(hy_s[:, hc], w_out_ref[0, hc, :])
        mix = part if mix is None else mix + part
    mix = mix + _dot(hy_s[:, D_ML:D_ML + D_CV], w_out_ref[0, D_ML:D_ML + D_CV, :])

    out_ref[0] = x + (mix * _rms_scale(mix)) * (g1 * post_g_ref[...])


def _const_spec(shape, index=None):
    index = index if index is not None else (0,) * len(shape)
    return pl.BlockSpec(shape, lambda b, t: index, pipeline_mode=pl.Buffered(1))


def _layer_spec(arr, l):
    return _const_spec((1,) + arr.shape[1:], (l,) + (0,) * (arr.ndim - 1))


def _mixer(l, x, mod_all, pre_g, post_g, w_main, w_glu, qkc_w, qkc_b, igate_b, fgate_b, mln_g,
           cv_w, cv_b, ln_g, ln_b, w_out):
    bsz, seq, d = x.shape
    T = SEQ_TILE
    gate_b = jnp.pad(jnp.concatenate([igate_b, fgate_b]), (0, LANES - N_GATE)).reshape(1, LANES)
    cv_w3 = jnp.pad(cv_w, ((0, 4 * SUBLANES - CV_KERNEL), (0, 0)))
    cv_w3 = cv_w3.reshape(4 * SUBLANES, N_CV_LT, LANES).transpose(1, 0, 2)
    cv_b3 = cv_b.reshape(N_CV_LT, 1, LANES)
    qkc_w8 = jnp.pad(qkc_w, ((0, SUBLANES - QK_CONV), (0, 0)))

    operands = [
        (x, pl.BlockSpec((1, T, d), lambda b, t: (b, t, 0))),
        (mod_all, pl.BlockSpec((1, 1, 1, N_MOD * d), lambda b, t: (l, b, 0, 0))),
        (pre_g.reshape(1, d), None),
        (post_g.reshape(1, d), None),
        (w_main, _layer_spec(w_main, l)),
        (w_glu, _layer_spec(w_glu, l)),
        (gate_b, None),
        (qkc_w8, None), (qkc_b.reshape(1, 2 * D_ML), None), (mln_g.reshape(1, D_ML), None),
        (cv_w3, None), (cv_b3, None), (ln_g.reshape(1, D_CV), None), (ln_b.reshape(1, D_CV), None),
        (w_out, _layer_spec(w_out, l)),
    ]
    args = [a for a, _ in operands]
    in_specs = [s if s is not None else _const_spec(a.shape) for a, s in operands]
    scratch = [
        pltpu.VMEM((T, d), _BF16),
        pltpu.VMEM((4 * D_ML // LANES, QK_HALO + T, LANES), _F32),
        pltpu.VMEM((T, D_ML), _BF16),
        pltpu.VMEM((D_ML, T), _F32),
        pltpu.VMEM((N_CV_LT, CV_HALO + T, LANES), _F32),
        pltpu.VMEM((N_CV_LT, T, LANES), _F32),
        pltpu.VMEM((T, D_ML + D_CV), _BF16),
        pltpu.VMEM((N_HEADS, HEAD_DIM, HEAD_DIM + LANES), _F32),
        pltpu.VMEM((N_HEADS, SUBLANES, LANES), _F32),
    ]
    return pl.pallas_call(
        _mixer_kernel,
        grid=(bsz, seq // T),
        in_specs=in_specs,
        out_specs=pl.BlockSpec((1, T, d), lambda b, t: (b, t, 0)),
        out_shape=jax.ShapeDtypeStruct(x.shape, x.dtype),
        scratch_shapes=scratch,
        compiler_params=pltpu.CompilerParams(
            dimension_semantics=("arbitrary", "arbitrary"),
            vmem_limit_bytes=VMEM_LIMIT_BYTES),
        name="mixer",
    )(*args)


def _ffn_kernel(x_ref, mod_ref, pre_g_ref, post_g_ref, up_ref, cw_ref, cb_ref, down_ref,
                out_ref, u_s, buf_s, halo_s, f_s):
    T = SEQ_TILE
    D = D_MODEL
    W = FFN_COL_BLOCK
    t_idx = pl.program_id(1)

    @pl.when(t_idx == 0)
    def _():
        halo_s[...] = jnp.zeros(halo_s.shape, _F32)

    mod = mod_ref[0, 0]
    sh2 = mod[:, 3 * D:4 * D]
    sc2 = mod[:, 4 * D:5 * D]
    g2 = mod[:, 5 * D:6 * D]
    pre_gain = pre_g_ref[...] * (1.0 + sc2)
    post_gain = g2 * post_g_ref[...]
    off = QK_HALO - (FFN_KERNEL - 1)
    n_lt = W // LANES

    x = x_ref[0]
    u_s[...] = ((x * _rms_scale(x)) * pre_gain + sh2).astype(_BF16)

    def up_conv(slot, part, col0):
        res = _dot(u_s[...], up_ref[0, :, col0:col0 + W])
        out = []
        for jj in range(n_lt):
            lt = col0 // LANES + jj
            cols = slice(lt * LANES, (lt + 1) * LANES)
            buf = buf_s.at[slot, part, jj]
            buf[0:QK_HALO, :] = halo_s[lt]
            buf[QK_HALO:QK_HALO + T, :] = res[:, jj * LANES:(jj + 1) * LANES]
            halo_s[lt] = buf[T:T + QK_HALO, :]
            acc = cb_ref[:, cols] + cw_ref[0:1, cols] * buf[off:off + T, :]
            for k in range(1, FFN_KERNEL):
                acc = acc + cw_ref[k:k + 1, cols] * buf[off + k:off + k + T, :]
            out.append(acc)
        return out

    for j in range(D_FF // W):
        a = up_conv(j % 2, 0, j * W)
        gt = up_conv(j % 2, 1, D_FF + j * W)
        for jj in range(n_lt):
            f_s[:, j * W + jj * LANES:j * W + (jj + 1) * LANES] = (
                _silu(gt[jj]) * a[jj]).astype(_BF16)

    f = _dot(f_s[...], down_ref[0])
    out_ref[0] = x + (f * _rms_scale(f)) * post_gain


def _ffn(l, x, mod_all, pre_g, post_g, up, conv_w, conv_b, down):
    bsz, seq, d = x.shape
    T = SEQ_TILE
    cw8 = jnp.pad(conv_w, ((0, SUBLANES - FFN_KERNEL), (0, 0)))
    operands = [
        (x, pl.BlockSpec((1, T, d), lambda b, t: (b, t, 0))),
        (mod_all, pl.BlockSpec((1, 1, 1, N_MOD * d), lambda b, t: (l, b, 0, 0))),
        (pre_g.reshape(1, d), None),
        (post_g.reshape(1, d), None),
        (up, _layer_spec(up, l)),
        (cw8, None),
        (conv_b.reshape(1, 2 * D_FF), None),
        (down, _layer_spec(down, l)),
    ]
    args = [a for a, _ in operands]
    in_specs = [s if s is not None else _const_spec(a.shape) for a, s in operands]
    scratch = [
        pltpu.VMEM((T, d), _BF16),
        pltpu.VMEM((2, 2, FFN_COL_BLOCK // LANES, QK_HALO + T, LANES), _F32),
        pltpu.VMEM((2 * D_FF // LANES, QK_HALO, LANES), _F32),
        pltpu.VMEM((T, D_FF), _BF16),
    ]
    return pl.pallas_call(
        _ffn_kernel,
        grid=(bsz, seq // T),
        in_specs=in_specs,
        out_specs=pl.BlockSpec((1, T, d), lambda b, t: (b, t, 0)),
        out_shape=jax.ShapeDtypeStruct(x.shape, x.dtype),
        scratch_shapes=scratch,
        compiler_params=pltpu.CompilerParams(
            dimension_semantics=("arbitrary", "arbitrary"),
            vmem_limit_bytes=VMEM_LIMIT_BYTES),
        name="ffn",
    )(*args)


def kernel(x, c, ada_w, ada_b, mix_pre_g, mix_post_g, w_in, qk_conv_w, qk_conv_b, igate_b, fgate_b, ml_norm_g, cv_dw_w, cv_dw_b, cv_ln_g, cv_ln_b, w_out, ffn_pre_g, ffn_post_g, ffn_up, ffn_conv_w, ffn_conv_b, ffn_down):
    bsz, seq, d = x.shape
    depth = ada_w.shape[0]
    assert d == D_MODEL and seq % SEQ_TILE == 0
    mod_all = _modulation(c, ada_w, ada_b).reshape(depth, bsz, 1, N_MOD * d)
    w_main = w_in.astype(_BF16)
    w_glu = w_main[:, :, GATE_COL + N_GATE:]
    w_out16 = w_out.astype(_BF16)
    up16 = ffn_up.astype(_BF16)
    down16 = ffn_down.astype(_BF16)
    for l in range(depth):
        x = _mixer(l, x, mod_all, mix_pre_g[l], mix_post_g[l], w_main, w_glu, qk_conv_w[l],
                   qk_conv_b[l], igate_b[l], fgate_b[l], ml_norm_g[l], cv_dw_w[l], cv_dw_b[l],
                   cv_ln_g[l], cv_ln_b[l], w_out16)
        x = _ffn(l, x, mod_all, ffn_pre_g[l], ffn_post_g[l], up16, ffn_conv_w[l],
                 ffn_conv_b[l], down16)
    return x
```
